```python
import jax, jax.numpy as jnp
from jax import lax
import numpy as np

D_MODEL = 2048
BATCH = 16
SEQ = 2048
DEPTH = 2

PLE_DIM = 256
FOX_HEADS = 16
FOX_HEAD_DIM = 64
FOX_W = FOX_HEADS * FOX_HEAD_DIM
Q_BLOCK = 128
RWKV_HEADS = 16
RWKV_HEAD_SIZE = 64
RWKV_W = RWKV_HEADS * RWKV_HEAD_SIZE
RWKV_DECAY_LORA = 64
RWKV_ICLR_LORA = 64
RWKV_SHIFT_COLS = 3 * RWKV_W + RWKV_DECAY_LORA + RWKV_ICLR_LORA
AB_COLS = 4 * FOX_W + FOX_HEADS + RWKV_SHIFT_COLS + RWKV_W
AB_OUT = FOX_W + RWKV_W
GMLP_W = D_MODEL
GMLP_GROUPS = 16
GMLP_GROUP_CH = GMLP_W // GMLP_GROUPS
GMLP_CHUNK = 128
C_COLS = 3 * GMLP_W
N_AB_LAYERS = (DEPTH + 1) // 2
N_C_LAYERS = DEPTH // 2
RMS_EPS = 1e-6
LN_EPS = 1e-5
RWKV_GN_EPS = 64e-5

kernel_name = 'hybrid_fox_rwkv7_gmlp_sandwich_ple'


def _split(x, sizes):
    offs = []
    o = 0
    for s in sizes[:-1]:
        o += s
        offs.append(o)
    return jnp.split(x, offs, axis=-1)


def rms_norm(x, g):
    xf = x.astype(jnp.float32)
    y = xf * lax.rsqrt(jnp.mean(xf * xf, axis=-1, keepdims=True) + RMS_EPS)
    return (y * g.astype(jnp.float32)).astype(x.dtype)


def layer_norm(x, g, b):
    xf = x.astype(jnp.float32)
    mu = jnp.mean(xf, axis=-1, keepdims=True)
    var = jnp.mean(jnp.square(xf - mu), axis=-1, keepdims=True)
    y = (xf - mu) * lax.rsqrt(var + LN_EPS) * g.astype(jnp.float32) + b.astype(jnp.float32)
    return y.astype(x.dtype)


def forgetting_attention(q, k, v, log_f):
    T = q.shape[1]
    c = jnp.cumsum(log_f, axis=1).transpose(0, 2, 1)
    scale = FOX_HEAD_DIM ** -0.5
    outs = []
    for blk in range(T // Q_BLOCK):
        q0 = blk * Q_BLOCK
        q1 = q0 + Q_BLOCK
        s = jnp.einsum('bqhd,bkhd->bhqk', q[:, q0:q1], k[:, :q1]).astype(jnp.float32) * scale
        bias = c[:, :, q0:q1, None] - c[:, :, None, :q1]
        causal = (q0 + jnp.arange(Q_BLOCK))[:, None] >= jnp.arange(q1)[None, :]
        s = jnp.where(causal, s + bias, -jnp.inf)
        pr = jax.nn.softmax(s, axis=-1)
        outs.append(jnp.einsum('bhqk,bkhd->bqhd', pr.astype(v.dtype), v[:, :q1]))
    return jnp.concatenate(outs, axis=1)


def rwkv7_time_mix(sh, mu, w0, w2, a0, a2, k_k, k_a, r_k, ln_g, ln_b):
    dt = sh.dtype
    B, T, _ = sh.shape
    H, N = RWKV_HEADS, RWKV_HEAD_SIZE
    prev = jnp.pad(sh, ((0, 0), (1, 0), (0, 0)))[:, :-1]
    sh = (sh + (prev - sh) * mu).astype(jnp.float32)
    r, k, v, w_lo, a_lo = _split(sh, (RWKV_W, RWKV_W, RWKV_W, RWKV_DECAY_LORA, RWKV_ICLR_LORA))
    w_raw = -jax.nn.softplus(-(w0.astype(jnp.float32) + jnp.tanh(w_lo) @ w2.astype(jnp.float32))) - 0.5
    decay = jnp.exp(-jnp.exp(w_raw))
    a = jax.nn.sigmoid(a0.astype(jnp.float32) + a_lo @ a2.astype(jnp.float32))
    kk = (k * k_k.astype(jnp.float32)).reshape(B, T, H, N)
    kk = kk / jnp.maximum(jnp.sqrt(jnp.sum(kk * kk, axis=-1, keepdims=True)), 1e-12)
    k = k * (1.0 + (a - 1.0) * k_a.astype(jnp.float32))
    heads = lambda z: z.reshape(B, T, H, N)
    r_h, k_h, v_h, a_h, w_h = heads(r), heads(k), heads(v), heads(a), heads(decay)
    xs = tuple(z.transpose(1, 0, 2, 3) for z in (r_h, w_h, k_h, v_h, kk, a_h))

    def step(S, inp):
        r_t, w_t, k_t, v_t, kk_t, a_t = inp
        sa = jnp.einsum('bhij,bhj->bhi', S, kk_t)
        S = S * w_t[:, :, None, :] - sa[..., None] * (kk_t * a_t)[:, :, None, :] + v_t[..., None] * k_t[:, :, None, :]
        y = jnp.einsum('bhij,bhj->bhi', S, r_t)
        return S, y

    S0 = jnp.zeros((B, H, N, N), jnp.float32)
    _, y = lax.scan(step, S0, xs)
    y = y.transpose(1, 0, 2, 3)
    mean = jnp.mean(y, axis=-1, keepdims=True)
    var = jnp.mean(jnp.square(y - mean), axis=-1, keepdims=True)
    y = ((y - mean) * lax.rsqrt(var + RWKV_GN_EPS)).reshape(B, T, RWKV_W)
    y = y * ln_g.astype(jnp.float32) + ln_b.astype(jnp.float32)
    bonus = jnp.sum(r_h * k_h * r_k.astype(jnp.float32), axis=-1, keepdims=True) * v_h
    y = y + bonus.reshape(B, T, RWKV_W)
    return y.astype(dt)


def chunked_spatial_gating(u, v, ln_g, ln_b, w_s, b_s):
    B, T, _ = v.shape
    u = jax.nn.gelu(u, approximate=False)
    v = layer_norm(jax.nn.gelu(v, approximate=False), ln_g, ln_b)
    vc = v.reshape(B, T // GMLP_CHUNK, GMLP_CHUNK, GMLP_GROUPS, GMLP_GROUP_CH)
    causal = jnp.tril(jnp.ones((GMLP_CHUNK, GMLP_CHUNK), w_s.dtype))
    mixed = jnp.einsum('gts,bnsgc->bntgc', w_s * causal, vc) + b_s.T[:, :, None]
    return u * mixed.reshape(B, T, GMLP_W)


def setup_inputs(seed: int = 0) -> dict:
    key = jax.random.key(seed)
    ks = iter(jax.random.split(key, 40))
    nrm = lambda shape, scale: scale * jax.random.normal(next(ks), shape, jnp.float32)
    uni = lambda shape, lo, hi: jax.random.uniform(next(ks), shape, jnp.float32, lo, hi)
    E, C = N_AB_LAYERS, N_C_LAYERS
    return {
        'x': nrm((BATCH, SEQ, D_MODEL), 1.0),
        'p': nrm((DEPTH, BATCH, SEQ, PLE_DIM), 1.0),
        'norm_pre': 1.0 + nrm((DEPTH, D_MODEL), 0.02),
        'norm_post': 1.0 + nrm((DEPTH, D_MODEL), 0.02),
        'ab_w_in': nrm((E, D_MODEL, AB_COLS), D_MODEL ** -0.5),
        'fox_f_bias': uni((E, FOX_HEADS), 1.0, 5.0),
        'rwkv_mu': uni((E, RWKV_SHIFT_COLS), 0.0, 1.0),
        'rwkv_w0': uni((E, RWKV_W), -6.0, 1.0),
        'rwkv_w2': nrm((E, RWKV_DECAY_LORA, RWKV_W), 0.1),
        'rwkv_a0': nrm((E, RWKV_W), 0.1),
        'rwkv_a2': nrm((E, RWKV_ICLR_LORA, RWKV_W), 0.1),
        'rwkv_k_k': 0.85 + nrm((E, RWKV_W), 0.05),
        'rwkv_k_a': 1.0 + nrm((E, RWKV_W), 0.05),
        'rwkv_r_k': nrm((E, RWKV_HEADS, RWKV_HEAD_SIZE), 0.1),
        'rwkv_ln_g': 1.0 + nrm((E, RWKV_W), 0.02),
        'rwkv_ln_b': nrm((E, RWKV_W), 0.02),
        'ab_w_out': nrm((E, AB_OUT, D_MODEL), AB_OUT ** -0.5),
        'c_w_in': nrm((C, D_MODEL, C_COLS), D_MODEL ** -0.5),
        'c_ln_g': 1.0 + nrm((C, GMLP_W), 0.02),
        'c_ln_b': nrm((C, GMLP_W), 0.02),
        'c_w_s': nrm((C, GMLP_GROUPS, GMLP_CHUNK, GMLP_CHUNK), GMLP_CHUNK ** -0.5),
        'c_b_s': 1.0 + nrm((C, GMLP_GROUPS, GMLP_CHUNK), 0.01),
        'c_w_out': nrm((C, GMLP_W, D_MODEL), GMLP_W ** -0.5),
        'ple_w_proj': nrm((DEPTH, PLE_DIM, D_MODEL), PLE_DIM ** -0.5),
        'ple_w_gate': nrm((DEPTH, D_MODEL, D_MODEL), D_MODEL ** -0.5),
    }


def reference(x, p, norm_pre, norm_post, ab_w_in, fox_f_bias, rwkv_mu, rwkv_w0, rwkv_w2, rwkv_a0, rwkv_a2,
              rwkv_k_k, rwkv_k_a, rwkv_r_k, rwkv_ln_g, rwkv_ln_b, ab_w_out, c_w_in, c_ln_g, c_ln_b, c_w_s,
              c_b_s, c_w_out, ple_w_proj, ple_w_gate):
    B, T, _ = x.shape
    h = x
    for i in range(DEPTH):
        j = i // 2
        xn = rms_norm(h, norm_pre[i])
        if i % 2 == 0:
            proj = xn @ ab_w_in[j]
            qa, ka, va, fa, ga, shb, gb = _split(
                proj, (FOX_W, FOX_W, FOX_W, FOX_HEADS, FOX_W, RWKV_SHIFT_COLS, RWKV_W))
            log_f = jax.nn.log_sigmoid((fa + fox_f_bias[j]).astype(jnp.float32))
            hd = lambda z: z.reshape(B, T, FOX_HEADS, FOX_HEAD_DIM)
            oa = forgetting_attention(hd(qa), hd(ka), hd(va), log_f).reshape(B, T, FOX_W)
            ob = rwkv7_time_mix(shb, rwkv_mu[j], rwkv_w0[j], rwkv_w2[j], rwkv_a0[j], rwkv_a2[j],
                                rwkv_k_k[j], rwkv_k_a[j], rwkv_r_k[j], rwkv_ln_g[j], rwkv_ln_b[j])
            y = jnp.concatenate([oa * jax.nn.silu(ga), ob * jax.nn.silu(gb)], axis=-1) @ ab_w_out[j]
        else:
            proj = xn @ c_w_in[j]
            u, v, g = _split(proj, (GMLP_W, GMLP_W, GMLP_W))
            oc = chunked_spatial_gating(u, v, c_ln_g[j], c_ln_b[j], c_w_s[j], c_b_s[j])
            y = (oc * jax.nn.silu(g)) @ c_w_out[j]
        h = h + rms_norm(y, norm_post[i])
        h = h + (p[i] @ ple_w_proj[i]) * jax.nn.sigmoid(h @ ple_w_gate[i])
    return h
```

```python
import functools

import jax
import jax.numpy as jnp
from jax import lax
from jax.experimental import pallas as pl
from jax.experimental.pallas import tpu as pltpu

F32 = jnp.float32
BF16 = jnp.bfloat16

D_MODEL = 2048
PLE_DIM = 256
HEADS = 16
HEAD_DIM = 64
HEAD_W = HEADS * HEAD_DIM
LORA = 64
GMLP_GROUPS = 16
GMLP_CHUNK = 128
RMS_EPS = 1e-6
LN_EPS = 1e-5
RWKV_GN_EPS = 64e-5
LANES = 128
HEADS_PER_BLOCK = LANES // HEAD_DIM
RWKV_CHUNK = 64
VMEM_LIMIT = 56 * 1024 * 1024

NT_DIMS = (((1,), (1,)), ((), ()))
TN_DIMS = (((0,), (0,)), ((), ()))


def _dot(a, b):
    return jnp.dot(a, b, preferred_element_type=F32)


def _split3(x):
    hi = x.astype(BF16)
    r1 = x - hi.astype(F32)
    mid = r1.astype(BF16)
    lo = (r1 - mid.astype(F32)).astype(BF16)
    return hi, mid, lo


def _dot_exact_lhs(a_bf16, x):
    hi, mid, lo = _split3(x)
    return _dot(a_bf16, hi) + _dot(a_bf16, mid) + _dot(a_bf16, lo)


def _silu(x):
    return x * jax.nn.sigmoid(x)


def _gelu(x):
    return 0.5 * x * (1.0 + lax.erf(x * (0.5 ** 0.5)))


def _norm_mm_kernel(x_ref, g_ref, w_ref, *rest, has_small):
    if has_small:
        ws_ref, o_ref, os_ref, xn_ref = rest
    else:
        o_ref, xn_ref = rest

    @pl.when(pl.program_id(1) == 0)
    def _():
        x = x_ref[...]
        ms = jnp.mean(x * x, axis=-1, keepdims=True)
        xn = (x * lax.rsqrt(ms + RMS_EPS) * g_ref[...]).astype(BF16)
        xn_ref[...] = xn
        if has_small:
            os_ref[...] = _dot(xn, ws_ref[...])

    o_ref[...] = _dot(xn_ref[...], w_ref[...]).astype(o_ref.dtype)


def _norm_matmul(x, g, w, w_small=None, *, tm=512, tn=1024):
    m, d = x.shape
    n = w.shape[1]
    has_small = w_small is not None
    in_specs = [
        pl.BlockSpec((tm, d), lambda i, j: (i, 0)),
        pl.BlockSpec((1, d), lambda i, j: (0, 0)),
        pl.BlockSpec((d, tn), lambda i, j: (0, j)),
    ]
    out_specs = [pl.BlockSpec((tm, tn), lambda i, j: (i, j))]
    out_shape = [jax.ShapeDtypeStruct((m, n), BF16)]
    args = [x, g, w]
    if has_small:
        ns = w_small.shape[1]
        in_specs.append(pl.BlockSpec((d, ns), lambda i, j: (0, 0)))
        out_specs.append(pl.BlockSpec((tm, ns), lambda i, j: (i, 0)))
        out_shape.append(jax.ShapeDtypeStruct((m, ns), F32))
        args.append(w_small)
    return pl.pallas_call(
        functools.partial(_norm_mm_kernel, has_small=has_small),
        grid=(m // tm, n // tn),
        in_specs=in_specs,
        out_specs=out_specs,
        out_shape=out_shape,
        scratch_shapes=[pltpu.VMEM((tm, d), BF16)],
        compiler_params=pltpu.CompilerParams(
            dimension_semantics=("parallel", "arbitrary"), vmem_limit_bytes=VMEM_LIMIT),
        name="norm_matmul_small" if has_small else "norm_matmul",
    )(*args)


CUMSUM_BLOCK = 256


def _fox_cumsum_kernel(f_ref, b_ref, ct_ref):
    t = f_ref.shape[1]
    row = lax.broadcasted_iota(jnp.int32, (CUMSUM_BLOCK, CUMSUM_BLOCK), 0)
    col = lax.broadcasted_iota(jnp.int32, (CUMSUM_BLOCK, CUMSUM_BLOCK), 1)
    tri = (row >= col).astype(BF16)
    carry = jnp.zeros((1, LANES), F32)
    for blk in range(t // CUMSUM_BLOCK):
        sl = slice(blk * CUMSUM_BLOCK, (blk + 1) * CUMSUM_BLOCK)
        log_f = jax.nn.log_sigmoid(f_ref[0, sl, :] + b_ref[...])
        c = _dot_exact_lhs(tri, log_f) + carry
        carry = c[CUMSUM_BLOCK - 1:CUMSUM_BLOCK, :]
        ct_ref[0, :, sl] = c.T[:HEADS, :]


def _fox_cumsum(small, f_bias_padded, batch, seq):
    return pl.pallas_call(
        _fox_cumsum_kernel,
        grid=(batch,),
        in_specs=[
            pl.BlockSpec((1, seq, LANES), lambda b: (b, 0, 1)),
            pl.BlockSpec((1, LANES), lambda b: (0, 0)),
        ],
        out_specs=pl.BlockSpec((1, HEADS, seq), lambda b: (b, 0, 0)),
        out_shape=jax.ShapeDtypeStruct((batch, HEADS, seq), F32),
        compiler_params=pltpu.CompilerParams(
            dimension_semantics=("parallel",), vmem_limit_bytes=VMEM_LIMIT),
        name="fox_cumsum",
    )(small, f_bias_padded)


MASK_VALUE = -1e30


def _fox_attn_kernel(q_ref, k_ref, v_ref, g_ref, ck_ref, o_ref, m_scr, l_scr, acc_scr, *, tq):
    qi = pl.program_id(2)
    ki = pl.program_id(3)

    @pl.when(ki == 0)
    def _():
        m_scr[...] = jnp.full(m_scr.shape, MASK_VALUE, F32)
        l_scr[...] = jnp.zeros(l_scr.shape, F32)
        acc_scr[...] = jnp.zeros(acc_scr.shape, F32)

    @pl.when(ki <= qi)
    def _():
        row = lax.broadcasted_iota(jnp.int32, (tq, tq), 0) + qi * tq
        col = lax.broadcasted_iota(jnp.int32, (tq, tq), 1) + ki * tq
        causal = row >= col
        for h in range(HEADS_PER_BLOCK):
            hs = slice(h * HEAD_DIM, (h + 1) * HEAD_DIM)
            s = lax.dot_general(q_ref[0, :, hs], k_ref[0, :, hs], NT_DIMS,
                                preferred_element_type=F32)
            s = s * (HEAD_DIM ** -0.5) - ck_ref[0, 0, h:h + 1, :]
            s = jnp.where(causal, s, MASK_VALUE)
            m_prev = m_scr[h]
            m_new = jnp.maximum(m_prev, jnp.max(s, axis=-1, keepdims=True))
            alpha = jnp.exp(m_prev - m_new)
            p = jnp.exp(s - m_new)
            l_scr[h] = alpha * l_scr[h] + jnp.sum(p, axis=-1, keepdims=True)
            acc_scr[h] = alpha * acc_scr[h] + _dot(p.astype(BF16), v_ref[0, :, hs])
            m_scr[h] = m_new

    @pl.when(ki == qi)
    def _():
        outs = [acc_scr[h] / l_scr[h] for h in range(HEADS_PER_BLOCK)]
        o = jnp.concatenate(outs, axis=-1)
        o_ref[0] = (o * _silu(g_ref[0].astype(F32))).astype(o_ref.dtype)


def _fox_attention(main, ct, batch, seq, *, tq=512):
    nblk = HEAD_W // LANES
    ct4 = ct.reshape(batch, nblk, HEADS_PER_BLOCK, seq)
    nq = seq // tq
    return pl.pallas_call(
        functools.partial(_fox_attn_kernel, tq=tq),
        grid=(batch, nblk, nq, nq),
        in_specs=[
            pl.BlockSpec((1, tq, LANES), lambda b, hp, qi, ki: (b, qi, hp)),
            pl.BlockSpec((1, tq, LANES), lambda b, hp, qi, ki: (b, jnp.minimum(ki, qi), nblk + hp)),
            pl.BlockSpec((1, tq, LANES), lambda b, hp, qi, ki: (b, jnp.minimum(ki, qi), 2 * nblk + hp)),
            pl.BlockSpec((1, tq, LANES), lambda b, hp, qi, ki: (b, qi, 3 * nblk + hp)),
            pl.BlockSpec((1, 1, HEADS_PER_BLOCK, tq),
                         lambda b, hp, qi, ki: (b, hp, 0, jnp.minimum(ki, qi))),
        ],
        out_specs=pl.BlockSpec((1, tq, LANES), lambda b, hp, qi, ki: (b, qi, hp)),
        out_shape=jax.ShapeDtypeStruct((batch, seq, HEAD_W), BF16),
        scratch_shapes=[
            pltpu.VMEM((HEADS_PER_BLOCK, tq, 1), F32),
            pltpu.VMEM((HEADS_PER_BLOCK, tq, 1), F32),
            pltpu.VMEM((HEADS_PER_BLOCK, tq, HEAD_DIM), F32),
        ],
        compiler_params=pltpu.CompilerParams(
            dimension_semantics=("parallel", "parallel", "parallel", "arbitrary"),
            vmem_limit_bytes=VMEM_LIMIT),
        name="fox_attention",
    )(main, main, main, main, ct4)


def _shift_lerp(x, prev_row, mu):
    rolled = pltpu.roll(x, 1, 0)
    first = lax.broadcasted_iota(jnp.int32, x.shape, 0) == 0
    x_prev = jnp.where(first, prev_row, rolled)
    return x + (x_prev - x) * mu


def _head_sum(x):
    parts = []
    for h in range(HEADS_PER_BLOCK):
        s = jnp.sum(x[:, h * HEAD_DIM:(h + 1) * HEAD_DIM], axis=-1, keepdims=True)
        parts.append(jnp.broadcast_to(s, (x.shape[0], HEAD_DIM)))
    return jnp.concatenate(parts, axis=-1)


def _unit_lower_inverse(l_strict):
    c = l_strict.shape[0]
    eye = (lax.broadcasted_iota(jnp.int32, (c, c), 0)
           == lax.broadcasted_iota(jnp.int32, (c, c), 1)).astype(F32)
    inv = eye + l_strict
    power = l_strict
    steps = c.bit_length() - 2
    for _ in range(steps):
        pb = power.astype(BF16)
        power = _dot(pb, pb)
        inv = inv + _dot(inv.astype(BF16), power.astype(BF16))
    return inv


def _rwkv_kernel(r_ref, k_ref, v_ref, g_ref, wa_ref, mu_ref, par_ref, w2_ref, a2_ref,
                 o_ref, state_scr, prev_scr, y_scr, *, tc):
    ti = pl.program_id(2)

    @pl.when(ti == 0)
    def _():
        state_scr[...] = jnp.zeros(state_scr.shape, F32)
        prev_scr[...] = jnp.zeros(prev_scr.shape, F32)

    r_in = r_ref[0].astype(F32)
    k_in = k_ref[0].astype(F32)
    v_in = v_ref[0].astype(F32)
    wa_in = wa_ref[0]
    r = _shift_lerp(r_in, prev_scr[0:1, :], mu_ref[0:1, :])
    k = _shift_lerp(k_in, prev_scr[1:2, :], mu_ref[1:2, :])
    v = _shift_lerp(v_in, prev_scr[2:3, :], mu_ref[2:3, :])
    wa = _shift_lerp(wa_in, prev_scr[3:4, :], mu_ref[3:4, :])
    prev_scr[0:1, :] = r_in[tc - 1:tc, :]
    prev_scr[1:2, :] = k_in[tc - 1:tc, :]
    prev_scr[2:3, :] = v_in[tc - 1:tc, :]
    prev_scr[3:4, :] = wa_in[tc - 1:tc, :]

    w0, a0, k_k, k_a, r_k, ln_g, ln_b = (par_ref[i:i + 1, :] for i in range(7))
    w_lo = jnp.tanh(wa[:, :LORA]).astype(BF16)
    a_lo = wa[:, LORA:].astype(BF16)
    w_raw = -jax.nn.softplus(-(w0 + _dot(w_lo, w2_ref[...]))) - 0.5
    logw = -jnp.exp(w_raw)
    a = jax.nn.sigmoid(a0 + _dot(a_lo, a2_ref[...]))
    kk = k * k_k
    kk = kk / jnp.maximum(jnp.sqrt(_head_sum(kk * kk)), 1e-12)
    k = k * (1.0 + (a - 1.0) * k_a)

    row = lax.broadcasted_iota(jnp.int32, (tc, tc), 0)
    col = lax.broadcasted_iota(jnp.int32, (tc, tc), 1)
    tri_blk = ((row >= col) & (row // RWKV_CHUNK == col // RWKV_CHUNK)).astype(BF16)
    cum = _dot_exact_lhs(tri_blk, logw)

    c = RWKV_CHUNK
    crow = lax.broadcasted_iota(jnp.int32, (c, c), 0)
    ccol = lax.broadcasted_iota(jnp.int32, (c, c), 1)
    lower_incl = crow >= ccol
    lower_strict = crow > ccol
    for ci in range(tc // c):
        rs = slice(ci * c, (ci + 1) * c)
        cum_c = cum[rs]
        logw_c = logw[rs]
        cum_end = cum_c[c - 1:c, :]
        e_pos = jnp.exp(cum_c)
        e_neg = jnp.exp(-cum_c)
        akk = a[rs] * kk[rs]
        r_t = (r[rs] * e_pos).astype(BF16)
        p_t = (-kk[rs] * jnp.exp(cum_c - logw_c)).astype(BF16)
        z_t = (akk * e_neg).astype(BF16)
        k_t = (k[rs] * e_neg).astype(BF16)
        e_end = jnp.exp(cum_end - cum_c)
        z_bar = (akk * e_end).astype(BF16)
        k_bar = (k[rs] * e_end).astype(BF16)
        g_end = jnp.exp(cum_end)
        v_c = v[rs].astype(BF16)
        for h in range(HEADS_PER_BLOCK):
            hs = slice(h * HEAD_DIM, (h + 1) * HEAD_DIM)
            s0 = state_scr[h]
            pr = jnp.concatenate([p_t[:, hs], r_t[:, hs]], axis=0)
            zk = jnp.concatenate([z_t[:, hs], k_t[:, hs]], axis=0)
            amat = lax.dot_general(pr, zk, NT_DIMS, preferred_element_type=F32)
            l_pz = jnp.where(lower_strict, amat[:c, :c], 0.0)
            l_pk = jnp.where(lower_strict, amat[:c, c:], 0.0)
            l_rz = jnp.where(lower_incl, amat[c:, :c], 0.0)
            l_rk = jnp.where(lower_incl, amat[c:, c:], 0.0)
            inv = _unit_lower_inverse(l_pz)
            ws = lax.dot_general(pr, s0.astype(BF16), NT_DIMS, preferred_element_type=F32)
            v_h = v_c[:, hs]
            rhs = ws[:c] + _dot(l_pk.astype(BF16), v_h)
            u = _dot(inv.astype(BF16), rhs.astype(BF16))
            u_b = u.astype(BF16)
            y = ws[c:] + _dot(l_rz.astype(BF16), u_b) + _dot(l_rk.astype(BF16), v_h)
            y_scr[rs, hs] = y
            upd = (lax.dot_general(u_b, z_bar[:, hs], TN_DIMS, preferred_element_type=F32)
                   + lax.dot_general(v_h, k_bar[:, hs], TN_DIMS, preferred_element_type=F32))
            state_scr[h] = s0 * g_end[:, hs] + upd

    y = y_scr[...]
    mean = _head_sum(y) * (1.0 / HEAD_DIM)
    yc = y - mean
    var = _head_sum(yc * yc) * (1.0 / HEAD_DIM)
    y = yc * lax.rsqrt(var + RWKV_GN_EPS) * ln_g + ln_b
    y = y + _head_sum(r * k * r_k) * v
    o_ref[0] = (y * _silu(g_ref[0].astype(F32))).astype(o_ref.dtype)


def _rwkv(main, small, mu_rows, par_rows, w2, a2, batch, seq, *, tc=256):
    nblk = HEAD_W // LANES
    return pl.pallas_call(
        functools.partial(_rwkv_kernel, tc=tc),
        grid=(batch, nblk, seq // tc),
        in_specs=[
            pl.BlockSpec((1, tc, LANES), lambda b, hp, ti: (b, ti, 4 * nblk + hp)),
            pl.BlockSpec((1, tc, LANES), lambda b, hp, ti: (b, ti, 5 * nblk + hp)),
            pl.BlockSpec((1, tc, LANES), lambda b, hp, ti: (b, ti, 6 * nblk + hp)),
            pl.BlockSpec((1, tc, LANES), lambda b, hp, ti: (b, ti, 7 * nblk + hp)),
            pl.BlockSpec((1, tc, LANES), lambda b, hp, ti: (b, ti, 0)),
            pl.BlockSpec((None, 8, LANES), lambda b, hp, ti: (hp, 0, 0)),
            pl.BlockSpec((None, 8, LANES), lambda b, hp, ti: (hp, 0, 0)),
            pl.BlockSpec((LORA, LANES), lambda b, hp, ti: (0, hp)),
            pl.BlockSpec((LORA, LANES), lambda b, hp, ti: (0, hp)),
        ],
        out_specs=pl.BlockSpec((1, tc, LANES), lambda b, hp, ti: (b, ti, hp)),
        out_shape=jax.ShapeDtypeStruct((batch, seq, HEAD_W), BF16),
        scratch_shapes=[
            pltpu.VMEM((HEADS_PER_BLOCK, HEAD_DIM, HEAD_DIM), F32),
            pltpu.VMEM((8, LANES), F32),
            pltpu.VMEM((tc, LANES), F32),
        ],
        compiler_params=pltpu.CompilerParams(
            dimension_semantics=("parallel", "parallel", "arbitrary"),
            vmem_limit_bytes=VMEM_LIMIT),
        name="rwkv7",
    )(main, main, main, main, small, mu_rows, par_rows, w2, a2)


def _sgu_kernel(u_ref, v_ref, g_ref, lng_ref, lnb_ref, ws_ref, bs_ref, o_ref, *, rows):
    c = GMLP_CHUNK
    causal = (lax.broadcasted_iota(jnp.int32, (c, c), 0)
              >= lax.broadcasted_iota(jnp.int32, (c, c), 1))
    v = _gelu(v_ref[...].astype(F32))
    mu = jnp.mean(v, axis=-1, keepdims=True)
    vc = v - mu
    var = jnp.mean(vc * vc, axis=-1, keepdims=True)
    vn = (vc * lax.rsqrt(var + LN_EPS) * lng_ref[...] + lnb_ref[...]).astype(BF16)
    for grp in range(GMLP_GROUPS):
        cs = slice(grp * c, (grp + 1) * c)
        w = jnp.where(causal, ws_ref[grp], 0.0).astype(BF16)
        bias = bs_ref[:, grp:grp + 1]
        for ch in range(rows // c):
            rs = slice(ch * c, (ch + 1) * c)
            mixed = _dot(w, vn[rs, cs]) + bias
            u = _gelu(u_ref[rs, cs].astype(F32))
            o_ref[rs, cs] = (u * mixed * _silu(g_ref[rs, cs].astype(F32))).astype(o_ref.dtype)


def _spatial_gating(proj, ln_g, ln_b, w_s, b_s_t, *, rows=256):
    m = proj.shape[0]
    d = D_MODEL
    return pl.pallas_call(
        functools.partial(_sgu_kernel, rows=rows),
        grid=(m // rows,),
        in_specs=[
            pl.BlockSpec((rows, d), lambda i: (i, 0)),
            pl.BlockSpec((rows, d), lambda i: (i, 1)),
            pl.BlockSpec((rows, d), lambda i: (i, 2)),
            pl.BlockSpec((1, d), lambda i: (0, 0)),
            pl.BlockSpec((1, d), lambda i: (0, 0)),
            pl.BlockSpec((GMLP_GROUPS, GMLP_CHUNK, GMLP_CHUNK), lambda i: (0, 0, 0)),
            pl.BlockSpec((GMLP_CHUNK, GMLP_GROUPS), lambda i: (0, 0)),
        ],
        out_specs=pl.BlockSpec((rows, d), lambda i: (i, 0)),
        out_shape=jax.ShapeDtypeStruct((m, d), BF16),
        compiler_params=pltpu.CompilerParams(
            dimension_semantics=("parallel",), vmem_limit_bytes=VMEM_LIMIT),
        name="spatial_gating",
    )(proj, proj, proj, ln_g, ln_b, w_s, b_s_t)


def _tail_kernel(a0_ref, a1_ref, h_ref, p_ref, w0_ref, w1_ref, gp_ref, wp_ref, wg_ref, o_ref):
    y = _dot(a0_ref[...], w0_ref[...]) + _dot(a1_ref[...], w1_ref[...])
    ms = jnp.mean(y * y, axis=-1, keepdims=True)
    h1 = h_ref[...] + y * lax.rsqrt(ms + RMS_EPS) * gp_ref[...]
    gate = _dot(h1.astype(BF16), wg_ref[...])
    pp = _dot(p_ref[...].astype(BF16), wp_ref[...])
    o_ref[...] = h1 + pp * jax.nn.sigmoid(gate)


def _tail(act0, blk0, act1, blk1, h, p, w_out, g_post, w_proj, w_gate, *, tm=256):
    m, d = h.shape
    half = w_out.shape[0] // 2
    return pl.pallas_call(
        _tail_kernel,
        grid=(m // tm,),
        in_specs=[
            pl.BlockSpec((tm, half), lambda i: (i, blk0)),
            pl.BlockSpec((tm, half), lambda i: (i, blk1)),
            pl.BlockSpec((tm, d), lambda i: (i, 0)),
            pl.BlockSpec((tm, PLE_DIM), lambda i: (i, 0)),
            pl.BlockSpec((half, d), lambda i: (0, 0)),
            pl.BlockSpec((half, d), lambda i: (1, 0)),
            pl.BlockSpec((1, d), lambda i: (0, 0)),
            pl.BlockSpec((PLE_DIM, d), lambda i: (0, 0)),
            pl.BlockSpec((d, d), lambda i: (0, 0)),
        ],
        out_specs=pl.BlockSpec((tm, d), lambda i: (i, 0)),
        out_shape=jax.ShapeDtypeStruct((m, d), F32),
        compiler_params=pltpu.CompilerParams(
            dimension_semantics=("parallel",), vmem_limit_bytes=VMEM_LIMIT),
        name="tail",
    )(act0, act1, h, p, w_out, w_out, g_post, w_proj, w_gate)


def _per_block_rows(vecs):
    nblk = HEAD_W // LANES
    rows = [v.reshape(nblk, 1, LANES) for v in vecs]
    rows += [jnp.zeros((nblk, 1, LANES), F32)] * (8 - len(rows))
    return jnp.concatenate(rows, axis=1)


def kernel(x, p, norm_pre, norm_post, ab_w_in, fox_f_bias, rwkv_mu, rwkv_w0, rwkv_w2, rwkv_a0, rwkv_a2, rwkv_k_k, rwkv_k_a, rwkv_r_k, rwkv_ln_g, rwkv_ln_b, ab_w_out, c_w_in, c_ln_g, c_ln_b, c_w_s, c_b_s, c_w_out, ple_w_proj, ple_w_gate):
    batch, seq, d = x.shape
    m = batch * seq
    w = HEAD_W
    h = x.reshape(m, d)

    w_in = ab_w_in[0]
    o_f = 3 * w
    o_ga = o_f + HEADS
    o_sh = o_ga + w
    o_lo = o_sh + 3 * w
    o_gb = o_lo + 2 * LORA
    w_main = jnp.concatenate(
        [w_in[:, :o_f], w_in[:, o_ga:o_sh], w_in[:, o_sh:o_lo], w_in[:, o_gb:]], axis=1).astype(BF16)
    w_small = jnp.concatenate(
        [w_in[:, o_lo:o_gb], w_in[:, o_f:o_ga], jnp.zeros((d, LANES - HEADS), F32)], axis=1).astype(BF16)
    main, small = _norm_matmul(h, norm_pre[0:1], w_main, w_small)
    main3 = main.reshape(batch, seq, 8 * w)
    small3 = small.reshape(batch, seq, 2 * LANES)

    f_bias = jnp.concatenate([fox_f_bias[0], jnp.zeros((LANES - HEADS,), F32)]).reshape(1, LANES)
    ct = _fox_cumsum(small3, f_bias, batch, seq)
    act_a = _fox_attention(main3, ct, batch, seq)

    mu = rwkv_mu[0]
    mu_rows = _per_block_rows([mu[:w], mu[w:2 * w], mu[2 * w:3 * w]])
    mu_rows = mu_rows.at[:, 3, :].set(jnp.broadcast_to(mu[3 * w:], (w // LANES, LANES)))
    par_rows = _per_block_rows([rwkv_w0[0], rwkv_a0[0], rwkv_k_k[0], rwkv_k_a[0],
                                rwkv_r_k[0].reshape(w), rwkv_ln_g[0], rwkv_ln_b[0]])
    act_b = _rwkv(main3, small3, mu_rows, par_rows, rwkv_w2[0].astype(BF16),
                  rwkv_a2[0].astype(BF16), batch, seq)

    h = _tail(act_a.reshape(m, w), 0, act_b.reshape(m, w), 0, h, p[0].reshape(m, PLE_DIM),
              ab_w_out[0].astype(BF16), norm_post[0:1], ple_w_proj[0].astype(BF16),
              ple_w_gate[0].astype(BF16))

    proj = _norm_matmul(h, norm_pre[1:2], c_w_in[0].astype(BF16))[0]
    act_c = _spatial_gating(proj, c_ln_g[0:1], c_ln_b[0:1], c_w_s[0], c_b_s[0].T)
    h = _tail(act_c, 0, act_c, 1, h, p[1].reshape(m, PLE_DIM),
              c_w_out[0].astype(BF16), norm_post[1:2], ple_w_proj[1].astype(BF16),
              ple_w_gate[1].astype(BF16))
    return h.reshape(batch, seq, d)
```

```python
import functools

import jax
import jax.numpy as jnp
from jax import lax
from jax.experimental import pallas as pl
from jax.experimental.pallas import tpu as pltpu

F32 = jnp.float32
BF16 = jnp.bfloat16

D_MODEL = 2048
PLE_DIM = 256
HEADS = 16
HEAD_DIM = 64
HEAD_W = HEADS * HEAD_DIM
LORA = 64
GMLP_GROUPS = 16
GMLP_CHUNK = 128
RMS_EPS = 1e-6
LN_EPS = 1e-5
RWKV_GN_EPS = 64e-5
LANES = 128
HEADS_PER_BLOCK = LANES // HEAD_DIM
RWKV_CHUNK = 64
VMEM_LIMIT = 56 * 1024 * 1024

NT_DIMS = (((1,), (1,)), ((), ()))
TN_DIMS = (((0,), (0,)), ((), ()))


def _dot(a, b):
    return jnp.dot(a, b, preferred_element_type=F32)


def _split(x, terms):
    parts = []
    for _ in range(terms - 1):
        part = x.astype(BF16)
        parts.append(part)
        x = x - part.astype(F32)
    parts.append(x.astype(BF16))
    return parts


def _dot_exact_lhs(a_bf16, x):
    return sum(_dot(a_bf16, part) for part in _split(x, 3))


def _silu(x):
    return x * jax.nn.sigmoid(x)


def _gelu(x):
    return 0.5 * x * (1.0 + lax.erf(x * (0.5 ** 0.5)))


def _norm_mm_kernel(x_ref, g_ref, w_ref, *rest, has_small):
    if has_small:
        ws_ref, o_ref, os_ref, xn_ref = rest
    else:
        o_ref, xn_ref = rest

    @pl.when(pl.program_id(1) == 0)
    def _():
        x = x_ref[...]
        ms = jnp.mean(x * x, axis=-1, keepdims=True)
        xn = (x * lax.rsqrt(ms + RMS_EPS) * g_ref[...]).astype(BF16)
        xn_ref[...] = xn
        if has_small:
            os_ref[...] = _dot(xn, ws_ref[...])

    o_ref[...] = _dot(xn_ref[...], w_ref[...]).astype(o_ref.dtype)


def _norm_matmul(x, g, w, w_small=None, *, tm=512, tn=1024):
    m, d = x.shape
    n = w.shape[1]
    has_small = w_small is not None
    in_specs = [
        pl.BlockSpec((tm, d), lambda i, j: (i, 0)),
        pl.BlockSpec((1, d), lambda i, j: (0, 0)),
        pl.BlockSpec((d, tn), lambda i, j: (0, j)),
    ]
    out_specs = [pl.BlockSpec((tm, tn), lambda i, j: (i, j))]
    out_shape = [jax.ShapeDtypeStruct((m, n), BF16)]
    args = [x, g, w]
    if has_small:
        ns = w_small.shape[1]
        in_specs.append(pl.BlockSpec((d, ns), lambda i, j: (0, 0)))
        out_specs.append(pl.BlockSpec((tm, ns), lambda i, j: (i, 0)))
        out_shape.append(jax.ShapeDtypeStruct((m, ns), F32))
        args.append(w_small)
    return pl.pallas_call(
        functools.partial(_norm_mm_kernel, has_small=has_small),
        grid=(m // tm, n // tn),
        in_specs=in_specs,
        out_specs=out_specs,
        out_shape=out_shape,
        scratch_shapes=[pltpu.VMEM((tm, d), BF16)],
        compiler_params=pltpu.CompilerParams(
            dimension_semantics=("parallel", "arbitrary"), vmem_limit_bytes=VMEM_LIMIT),
        name="norm_matmul_small" if has_small else "norm_matmul",
    )(*args)


CUMSUM_BLOCK = 256


def _fox_cumsum_kernel(f_ref, b_ref, ct_ref):
    t = f_ref.shape[1]
    row = lax.broadcasted_iota(jnp.int32, (CUMSUM_BLOCK, CUMSUM_BLOCK), 0)
    col = lax.broadcasted_iota(jnp.int32, (CUMSUM_BLOCK, CUMSUM_BLOCK), 1)
    tri = (row >= col).astype(BF16)
    carry = jnp.zeros((1, LANES), F32)
    for blk in range(t // CUMSUM_BLOCK):
        sl = slice(blk * CUMSUM_BLOCK, (blk + 1) * CUMSUM_BLOCK)
        log_f = jax.nn.log_sigmoid(f_ref[0, sl, :] + b_ref[...])
        c = _dot_exact_lhs(tri, log_f) + carry
        carry = c[CUMSUM_BLOCK - 1:CUMSUM_BLOCK, :]
        ct_ref[0, :, sl] = c.T[:HEADS, :]


def _fox_cumsum(small, f_bias_padded, batch, seq):
    return pl.pallas_call(
        _fox_cumsum_kernel,
        grid=(batch,),
        in_specs=[
            pl.BlockSpec((1, seq, LANES), lambda b: (b, 0, 1)),
            pl.BlockSpec((1, LANES), lambda b: (0, 0)),
        ],
        out_specs=pl.BlockSpec((1, HEADS, seq), lambda b: (b, 0, 0)),
        out_shape=jax.ShapeDtypeStruct((batch, HEADS, seq), F32),
        compiler_params=pltpu.CompilerParams(
            dimension_semantics=("parallel",), vmem_limit_bytes=VMEM_LIMIT),
        name="fox_cumsum",
    )(small, f_bias_padded)


MASK_VALUE = -1e30


def _fox_attn_kernel(q_ref, k_ref, v_ref, g_ref, ck_ref, o_ref, m_scr, l_scr, acc_scr, *, tq):
    qi = pl.program_id(2)
    ki = pl.program_id(3)

    @pl.when(ki == 0)
    def _():
        m_scr[...] = jnp.full(m_scr.shape, MASK_VALUE, F32)
        l_scr[...] = jnp.zeros(l_scr.shape, F32)
        acc_scr[...] = jnp.zeros(acc_scr.shape, F32)

    @pl.when(ki <= qi)
    def _():
        row = lax.broadcasted_iota(jnp.int32, (tq, tq), 0) + qi * tq
        col = lax.broadcasted_iota(jnp.int32, (tq, tq), 1) + ki * tq
        causal = row >= col
        for h in range(HEADS_PER_BLOCK):
            hs = slice(h * HEAD_DIM, (h + 1) * HEAD_DIM)
            s = lax.dot_general(q_ref[0, :, hs], k_ref[0, :, hs], NT_DIMS,
                                preferred_element_type=F32)
            s = s * (HEAD_DIM ** -0.5) - ck_ref[0, 0, h:h + 1, :]
            s = jnp.where(causal, s, MASK_VALUE)
            m_prev = m_scr[h]
            m_new = jnp.maximum(m_prev, jnp.max(s, axis=-1, keepdims=True))
            alpha = jnp.exp(m_prev - m_new)
            p = jnp.exp(s - m_new)
            l_scr[h] = alpha * l_scr[h] + jnp.sum(p, axis=-1, keepdims=True)
            acc_scr[h] = alpha * acc_scr[h] + _dot(p.astype(BF16), v_ref[0, :, hs])
            m_scr[h] = m_new

    @pl.when(ki == qi)
    def _():
        outs = [acc_scr[h] / l_scr[h] for h in range(HEADS_PER_BLOCK)]
        o = jnp.concatenate(outs, axis=-1)
        o_ref[0] = (o * _silu(g_ref[0].astype(F32))).astype(o_ref.dtype)


def _fox_attention(main, ct, batch, seq, *, tq=512):
    nblk = HEAD_W // LANES
    ct4 = ct.reshape(batch, nblk, HEADS_PER_BLOCK, seq)
    nq = seq // tq
    return pl.pallas_call(
        functools.partial(_fox_attn_kernel, tq=tq),
        grid=(batch, nblk, nq, nq),
        in_specs=[
            pl.BlockSpec((1, tq, LANES), lambda b, hp, qi, ki: (b, qi, hp)),
            pl.BlockSpec((1, tq, LANES), lambda b, hp, qi, ki: (b, jnp.minimum(ki, qi), nblk + hp)),
            pl.BlockSpec((1, tq, LANES), lambda b, hp, qi, ki: (b, jnp.minimum(ki, qi), 2 * nblk + hp)),
            pl.BlockSpec((1, tq, LANES), lambda b, hp, qi, ki: (b, qi, 3 * nblk + hp)),
            pl.BlockSpec((1, 1, HEADS_PER_BLOCK, tq),
                         lambda b, hp, qi, ki: (b, hp, 0, jnp.minimum(ki, qi))),
        ],
        out_specs=pl.BlockSpec((1, tq, LANES), lambda b, hp, qi, ki: (b, qi, hp)),
        out_shape=jax.ShapeDtypeStruct((batch, seq, HEAD_W), BF16),
        scratch_shapes=[
            pltpu.VMEM((HEADS_PER_BLOCK, tq, 1), F32),
            pltpu.VMEM((HEADS_PER_BLOCK, tq, 1), F32),
            pltpu.VMEM((HEADS_PER_BLOCK, tq, HEAD_DIM), F32),
        ],
        compiler_params=pltpu.CompilerParams(
            dimension_semantics=("parallel", "parallel", "parallel", "arbitrary"),
            vmem_limit_bytes=VMEM_LIMIT),
        name="fox_attention",
    )(main, main, main, main, ct4)


def _shift_lerp(x, prev_row, mu):
    rolled = pltpu.roll(x, 1, 0)
    first = lax.broadcasted_iota(jnp.int32, x.shape, 0) == 0
    x_prev = jnp.where(first, prev_row, rolled)
    return x + (x_prev - x) * mu


def _head_sum(x, terms):
    row = lax.broadcasted_iota(jnp.int32, (LANES, LANES), 0)
    col = lax.broadcasted_iota(jnp.int32, (LANES, LANES), 1)
    same_head = (row // HEAD_DIM == col // HEAD_DIM).astype(BF16)
    return sum(_dot(part, same_head) for part in _split(x, terms))


def _stack_heads(x):
    lane = lax.broadcasted_iota(jnp.int32, x.shape, x.ndim - 1)
    return jnp.concatenate(
        [jnp.where(lane < HEAD_DIM, x, 0.0), jnp.where(lane >= HEAD_DIM, x, 0.0)], axis=x.ndim - 2)


def _bdot(a, b):
    return lax.dot_general(a, b, (((2,), (1,)), ((0,), (0,))), preferred_element_type=F32)


def _bdot_nt(a, b):
    return lax.dot_general(a, b, (((2,), (2,)), ((0,), (0,))), preferred_element_type=F32)


def _bdot_tn(a, b):
    return lax.dot_general(a, b, (((1,), (1,)), ((0,), (0,))), preferred_element_type=F32)


def _unit_lower_inverse(l_strict, order):
    n = l_strict.shape[-1]
    eye = (lax.broadcasted_iota(jnp.int32, (n, n), 0)
           == lax.broadcasted_iota(jnp.int32, (n, n), 1)).astype(F32)
    inv = eye + l_strict
    power = l_strict.astype(BF16)
    power = _bdot(power, power).astype(BF16)
    last = order.bit_length() - 2
    for k in range(1, last):
        both = _bdot(jnp.concatenate([inv.astype(BF16), power], axis=1), power)
        inv = inv + both[:, :n]
        power = both[:, n:].astype(BF16)
    return inv + _bdot(inv.astype(BF16), power)


def _rwkv_kernel(r_ref, k_ref, v_ref, g_ref, wa_ref, mu_ref, par_ref, w2_ref, a2_ref,
                 o_ref, state_scr, prev_scr, *, tc):
    ti = pl.program_id(2)

    @pl.when(ti == 0)
    def _():
        state_scr[...] = jnp.zeros(state_scr.shape, F32)
        prev_scr[...] = jnp.zeros(prev_scr.shape, F32)

    r_in = r_ref[0].astype(F32)
    k_in = k_ref[0].astype(F32)
    v_in = v_ref[0].astype(F32)
    wa_in = wa_ref[0]
    r = _shift_lerp(r_in, prev_scr[0:1, :], mu_ref[0:1, :])
    k = _shift_lerp(k_in, prev_scr[1:2, :], mu_ref[1:2, :])
    v = _shift_lerp(v_in, prev_scr[2:3, :], mu_ref[2:3, :])
    wa = _shift_lerp(wa_in, prev_scr[3:4, :], mu_ref[3:4, :])
    prev_scr[0:1, :] = r_in[tc - 1:tc, :]
    prev_scr[1:2, :] = k_in[tc - 1:tc, :]
    prev_scr[2:3, :] = v_in[tc - 1:tc, :]
    prev_scr[3:4, :] = wa_in[tc - 1:tc, :]

    w0, a0, k_k, k_a, r_k, ln_g, ln_b = (par_ref[i:i + 1, :] for i in range(7))
    w_lo = jnp.tanh(wa[:, :LORA]).astype(BF16)
    a_lo = wa[:, LORA:].astype(BF16)
    w_raw = -jax.nn.softplus(-(w0 + _dot(w_lo, w2_ref[...]))) - 0.5
    logw = -jnp.exp(w_raw)
    a = jax.nn.sigmoid(a0 + _dot(a_lo, a2_ref[...]))
    kk = k * k_k
    kk = kk / jnp.maximum(jnp.sqrt(_head_sum(kk * kk, 2)), 1e-12)
    k = k * (1.0 + (a - 1.0) * k_a)

    row = lax.broadcasted_iota(jnp.int32, (LANES, LANES), 0)
    col = lax.broadcasted_iota(jnp.int32, (LANES, LANES), 1)
    tri_blk = ((row >= col) & (row // RWKV_CHUNK == col // RWKV_CHUNK)).astype(BF16)
    cum = jnp.concatenate(
        [_dot_exact_lhs(tri_blk, logw[i:i + LANES]) for i in range(0, tc, LANES)], axis=0)

    c = RWKV_CHUNK
    nc = tc // c
    n2 = HEADS_PER_BLOCK * c
    srow = lax.broadcasted_iota(jnp.int32, (n2, n2), 0)
    scol = lax.broadcasted_iota(jnp.int32, (n2, n2), 1)
    lower_incl = srow >= scol
    lower_strict = srow > scol

    def chunked(x):
        return x.reshape(nc, c, LANES)

    cum3 = chunked(cum)
    cum_end = cum3[:, c - 1:c, :]
    e_neg = jnp.exp(-cum3)
    e_end = jnp.exp(cum_end - cum3)
    akk = chunked(a * kk)
    k3 = chunked(k)
    r_s = _stack_heads(chunked(r) * jnp.exp(cum3))
    p_s = _stack_heads(-chunked(kk) * jnp.exp(cum3 - chunked(logw))).astype(BF16)
    z_s = _stack_heads(akk * e_neg).astype(BF16)
    k_s = _stack_heads(k3 * e_neg).astype(BF16)
    zbar_s = _stack_heads(akk * e_end).astype(BF16)
    kbar_s = _stack_heads(k3 * e_end).astype(BF16)
    v_s = _stack_heads(chunked(v)).astype(BF16)
    g_end = jnp.exp(cum_end)
    pr = jnp.concatenate([p_s, r_s.astype(BF16)], axis=1)
    zk = jnp.concatenate([z_s, k_s], axis=1)
    amat = _bdot_nt(pr, zk)
    l_pz = jnp.where(lower_strict, amat[:, :n2, :n2], 0.0)
    l_pk = jnp.where(lower_strict, amat[:, :n2, n2:], 0.0).astype(BF16)
    l_rz = jnp.where(lower_incl, amat[:, n2:, :n2], 0.0).astype(BF16)
    l_rk = jnp.where(lower_incl, amat[:, n2:, n2:], 0.0).astype(BF16)
    inv = _unit_lower_inverse(l_pz, c).astype(BF16)
    lv = _bdot(l_pk, v_s).astype(BF16)
    tp_ui = _bdot(inv, jnp.concatenate([p_s, lv], axis=2)).astype(BF16)
    rz = _bdot(l_rz, tp_ui)
    qeff = (r_s + rz[:, :, :LANES]).astype(BF16)
    yi = rz[:, :, LANES:] + _bdot(l_rk, v_s)
    mz = _bdot_tn(tp_ui, zbar_s)
    m_low = mz[:, :LANES].astype(BF16)
    b_mat = mz[:, LANES:] + _bdot_tn(v_s, kbar_s)

    s = state_scr[...]
    ys = []
    for ci in range(nc):
        s_b = s.astype(BF16)
        y_s = lax.dot_general(qeff[ci], s_b, NT_DIMS, preferred_element_type=F32) + yi[ci]
        ys.append(y_s[:c] + y_s[c:])
        s = s * g_end[ci] + _dot(s_b, m_low[ci]) + b_mat[ci]
    state_scr[...] = s

    y = jnp.concatenate(ys, axis=0)
    mean = _head_sum(y, 2) * (1.0 / HEAD_DIM)
    yc = y - mean
    var = _head_sum(yc * yc, 2) * (1.0 / HEAD_DIM)
    y = yc * lax.rsqrt(var + RWKV_GN_EPS) * ln_g + ln_b
    y = y + _head_sum(r * k * r_k, 2) * v
    o_ref[0] = (y * _silu(g_ref[0].astype(F32))).astype(o_ref.dtype)


def _rwkv(main, small, mu_rows, par_rows, w2, a2, batch, seq, *, tc=512):
    nblk = HEAD_W // LANES
    return pl.pallas_call(
        functools.partial(_rwkv_kernel, tc=tc),
        grid=(batch, nblk, seq // tc),
        in_specs=[
            pl.BlockSpec((1, tc, LANES), lambda b, hp, ti: (b, ti, 4 * nblk + hp)),
            pl.BlockSpec((1, tc, LANES), lambda b, hp, ti: (b, ti, 5 * nblk + hp)),
            pl.BlockSpec((1, tc, LANES), lambda b, hp, ti: (b, ti, 6 * nblk + hp)),
            pl.BlockSpec((1, tc, LANES), lambda b, hp, ti: (b, ti, 7 * nblk + hp)),
            pl.BlockSpec((1, tc, LANES), lambda b, hp, ti: (b, ti, 0)),
            pl.BlockSpec((None, 8, LANES), lambda b, hp, ti: (hp, 0, 0)),
            pl.BlockSpec((None, 8, LANES), lambda b, hp, ti: (hp, 0, 0)),
            pl.BlockSpec((LORA, LANES), lambda b, hp, ti: (0, hp)),
            pl.BlockSpec((LORA, LANES), lambda b, hp, ti: (0, hp)),
        ],
        out_specs=pl.BlockSpec((1, tc, LANES), lambda b, hp, ti: (b, ti, hp)),
        out_shape=jax.ShapeDtypeStruct((batch, seq, HEAD_W), BF16),
        scratch_shapes=[
            pltpu.VMEM((LANES, LANES), F32),
            pltpu.VMEM((8, LANES), F32),
        ],
        compiler_params=pltpu.CompilerParams(
            dimension_semantics=("parallel", "parallel", "arbitrary"),
            vmem_limit_bytes=VMEM_LIMIT),
        name="rwkv7",
    )(main, main, main, main, small, mu_rows, par_rows, w2, a2)


def _sgu_kernel(u_ref, v_ref, g_ref, lng_ref, lnb_ref, ws_ref, bs_ref, o_ref, *, rows):
    c = GMLP_CHUNK
    causal = (lax.broadcasted_iota(jnp.int32, (c, c), 0)
              >= lax.broadcasted_iota(jnp.int32, (c, c), 1))
    v = _gelu(v_ref[...].astype(F32))
    mu = jnp.mean(v, axis=-1, keepdims=True)
    vc = v - mu
    var = jnp.mean(vc * vc, axis=-1, keepdims=True)
    vn = (vc * lax.rsqrt(var + LN_EPS) * lng_ref[...] + lnb_ref[...]).astype(BF16)
    for grp in range(GMLP_GROUPS):
        cs = slice(grp * c, (grp + 1) * c)
        w = jnp.where(causal, ws_ref[grp], 0.0).astype(BF16)
        bias = bs_ref[:, grp:grp + 1]
        for ch in range(rows // c):
            rs = slice(ch * c, (ch + 1) * c)
            mixed = _dot(w, vn[rs, cs]) + bias
            u = _gelu(u_ref[rs, cs].astype(F32))
            o_ref[rs, cs] = (u * mixed * _silu(g_ref[rs, cs].astype(F32))).astype(o_ref.dtype)


def _spatial_gating(proj, ln_g, ln_b, w_s, b_s_t, *, rows=256):
    m = proj.shape[0]
    d = D_MODEL
    return pl.pallas_call(
        functools.partial(_sgu_kernel, rows=rows),
        grid=(m // rows,),
        in_specs=[
            pl.BlockSpec((rows, d), lambda i: (i, 0)),
            pl.BlockSpec((rows, d), lambda i: (i, 1)),
            pl.BlockSpec((rows, d), lambda i: (i, 2)),
            pl.BlockSpec((1, d), lambda i: (0, 0)),
            pl.BlockSpec((1, d), lambda i: (0, 0)),
            pl.BlockSpec((GMLP_GROUPS, GMLP_CHUNK, GMLP_CHUNK), lambda i: (0, 0, 0)),
            pl.BlockSpec((GMLP_CHUNK, GMLP_GROUPS), lambda i: (0, 0)),
        ],
        out_specs=pl.BlockSpec((rows, d), lambda i: (i, 0)),
        out_shape=jax.ShapeDtypeStruct((m, d), BF16),
        compiler_params=pltpu.CompilerParams(
            dimension_semantics=("parallel",), vmem_limit_bytes=VMEM_LIMIT),
        name="spatial_gating",
    )(proj, proj, proj, ln_g, ln_b, w_s, b_s_t)


def _tail_kernel(a0_ref, a1_ref, h_ref, p_ref, w0_ref, w1_ref, gp_ref, wp_ref, wg_ref, o_ref):
    y = _dot(a0_ref[...], w0_ref[...]) + _dot(a1_ref[...], w1_ref[...])
    ms = jnp.mean(y * y, axis=-1, keepdims=True)
    h1 = h_ref[...] + y * lax.rsqrt(ms + RMS_EPS) * gp_ref[...]
    gate = _dot(h1.astype(BF16), wg_ref[...])
    pp = _dot(p_ref[...].astype(BF16), wp_ref[...])
    o_ref[...] = h1 + pp * jax.nn.sigmoid(gate)


def _tail(act0, blk0, act1, blk1, h, p, w_out, g_post, w_proj, w_gate, *, tm=256):
    m, d = h.shape
    half = w_out.shape[0] // 2
    return pl.pallas_call(
        _tail_kernel,
        grid=(m // tm,),
        in_specs=[
            pl.BlockSpec((tm, half), lambda i: (i, blk0)),
            pl.BlockSpec((tm, half), lambda i: (i, blk1)),
            pl.BlockSpec((tm, d), lambda i: (i, 0)),
            pl.BlockSpec((tm, PLE_DIM), lambda i: (i, 0)),
            pl.BlockSpec((half, d), lambda i: (0, 0)),
            pl.BlockSpec((half, d), lambda i: (1, 0)),
            pl.BlockSpec((1, d), lambda i: (0, 0)),
            pl.BlockSpec((PLE_DIM, d), lambda i: (0, 0)),
            pl.BlockSpec((d, d), lambda i: (0, 0)),
        ],
        out_specs=pl.BlockSpec((tm, d), lambda i: (i, 0)),
        out_shape=jax.ShapeDtypeStruct((m, d), F32),
        compiler_params=pltpu.CompilerParams(
            dimension_semantics=("parallel",), vmem_limit_bytes=VMEM_LIMIT),
        name="tail",
    )(act0, act1, h, p, w_out, w_out, g_post, w_proj, w_gate)


def _per_block_rows(vecs):
    nblk = HEAD_W // LANES
    rows = [v.reshape(nblk, 1, LANES) for v in vecs]
    rows += [jnp.zeros((nblk, 1, LANES), F32)] * (8 - len(rows))
    return jnp.concatenate(rows, axis=1)


def kernel(x, p, norm_pre, norm_post, ab_w_in, fox_f_bias, rwkv_mu, rwkv_w0, rwkv_w2, rwkv_a0, rwkv_a2, rwkv_k_k, rwkv_k_a, rwkv_r_k, rwkv_ln_g, rwkv_ln_b, ab_w_out, c_w_in, c_ln_g, c_ln_b, c_w_s, c_b_s, c_w_out, ple_w_proj, ple_w_gate):
    batch, seq, d = x.shape
    m = batch * seq
    w = HEAD_W
    h = x.reshape(m, d)

    w_in = ab_w_in[0]
    o_f = 3 * w
    o_ga = o_f + HEADS
    o_sh = o_ga + w
    o_lo = o_sh + 3 * w
    o_gb = o_lo + 2 * LORA
    w_main = jnp.concatenate(
        [w_in[:, :o_f], w_in[:, o_ga:o_sh], w_in[:, o_sh:o_lo], w_in[:, o_gb:]], axis=1).astype(BF16)
    w_small = jnp.concatenate(
        [w_in[:, o_lo:o_gb], w_in[:, o_f:o_ga], jnp.zeros((d, LANES - HEADS), F32)], axis=1).astype(BF16)
    main, small = _norm_matmul(h, norm_pre[0:1], w_main, w_small)
    main3 = main.reshape(batch, seq, 8 * w)
    small3 = small.reshape(batch, seq, 2 * LANES)

    f_bias = jnp.concatenate([fox_f_bias[0], jnp.zeros((LANES - HEADS,), F32)]).reshape(1, LANES)
    ct = _fox_cumsum(small3, f_bias, batch, seq)
    act_a = _fox_attention(main3, ct, batch, seq)

    mu = rwkv_mu[0]
    mu_rows = _per_block_rows([mu[:w], mu[w:2 * w], mu[2 * w:3 * w]])
    mu_rows = mu_rows.at[:, 3, :].set(jnp.broadcast_to(mu[3 * w:], (w // LANES, LANES)))
    par_rows = _per_block_rows([rwkv_w0[0], rwkv_a0[0], rwkv_k_k[0], rwkv_k_a[0],
                                rwkv_r_k[0].reshape(w), rwkv_ln_g[0], rwkv_ln_b[0]])
    act_b = _rwkv(main3, small3, mu_rows, par_rows, rwkv_w2[0].astype(BF16),
                  rwkv_a2[0].astype(BF16), batch, seq)

    h = _tail(act_a.reshape(m, w), 0, act_b.reshape(m, w), 0, h, p[0].reshape(m, PLE_DIM),
              ab_w_out[0].astype(BF16), norm_post[0:1], ple_w_proj[0].astype(BF16),
              ple_w_gate[0].astype(BF16))

    proj = _norm_matmul(h, norm_pre[1:2], c_w_in[0].astype(BF16))[0]
    act_c = _spatial_gating(proj, c_ln_g[0:1], c_ln_b[0:1], c_w_s[0], c_b_s[0].T)
    h = _tail(act_c, 0, act_c, 1, h, p[1].reshape(m, PLE_DIM),
              c_w_out[0].astype(BF16), norm_post[1:2], ple_w_proj[1].astype(BF16),
              ple_w_gate[1].astype(BF16))
    return h.reshape(batch, seq, d)
```

```python
import functools

import jax
import jax.numpy as jnp
from jax import lax
from jax.experimental import pallas as pl
from jax.experimental.pallas import tpu as pltpu

F32 = jnp.float32
BF16 = jnp.bfloat16

D_MODEL = 2048
PLE_DIM = 256
HEADS = 16
HEAD_DIM = 64
HEAD_W = HEADS * HEAD_DIM
LORA = 64
GMLP_GROUPS = 16
GMLP_CHUNK = 128
RMS_EPS = 1e-6
LN_EPS = 1e-5
RWKV_GN_EPS = 64e-5
LANES = 128
HEADS_PER_BLOCK = LANES // HEAD_DIM
RWKV_CHUNK = 64
VMEM_LIMIT = 56 * 1024 * 1024

NT_DIMS = (((1,), (1,)), ((), ()))
TN_DIMS = (((0,), (0,)), ((), ()))


def _dot(a, b):
    return jnp.dot(a, b, preferred_element_type=F32)


def _split(x, terms):
    parts = []
    for _ in range(terms - 1):
        part = x.astype(BF16)
        parts.append(part)
        x = x - part.astype(F32)
    parts.append(x.astype(BF16))
    return parts


def _dot_exact_lhs(a_bf16, x):
    return sum(_dot(a_bf16, part) for part in _split(x, 3))


def _silu(x):
    return x * jax.nn.sigmoid(x)


def _gelu(x):
    return 0.5 * x * (1.0 + lax.erf(x * (0.5 ** 0.5)))


def _norm_mm_kernel(x_ref, g_ref, w_ref, *rest, has_small):
    if has_small:
        ws_ref, o_ref, os_ref, xn_ref = rest
    else:
        o_ref, xn_ref = rest

    @pl.when(pl.program_id(1) == 0)
    def _():
        x = x_ref[...]
        ms = jnp.mean(x * x, axis=-1, keepdims=True)
        xn = (x * lax.rsqrt(ms + RMS_EPS) * g_ref[...]).astype(BF16)
        xn_ref[...] = xn
        if has_small:
            os_ref[...] = _dot(xn, ws_ref[...])

    o_ref[...] = _dot(xn_ref[...], w_ref[...]).astype(o_ref.dtype)


def _norm_matmul(x, g, w, w_small=None, *, tm=512, tn=1024):
    m, d = x.shape
    n = w.shape[1]
    has_small = w_small is not None
    in_specs = [
        pl.BlockSpec((tm, d), lambda i, j: (i, 0)),
        pl.BlockSpec((1, d), lambda i, j: (0, 0)),
        pl.BlockSpec((d, tn), lambda i, j: (0, j)),
    ]
    out_specs = [pl.BlockSpec((tm, tn), lambda i, j: (i, j))]
    out_shape = [jax.ShapeDtypeStruct((m, n), BF16)]
    args = [x, g, w]
    if has_small:
        ns = w_small.shape[1]
        in_specs.append(pl.BlockSpec((d, ns), lambda i, j: (0, 0)))
        out_specs.append(pl.BlockSpec((tm, ns), lambda i, j: (i, 0)))
        out_shape.append(jax.ShapeDtypeStruct((m, ns), F32))
        args.append(w_small)
    return pl.pallas_call(
        functools.partial(_norm_mm_kernel, has_small=has_small),
        grid=(m // tm, n // tn),
        in_specs=in_specs,
        out_specs=out_specs,
        out_shape=out_shape,
        scratch_shapes=[pltpu.VMEM((tm, d), BF16)],
        compiler_params=pltpu.CompilerParams(
            dimension_semantics=("parallel", "arbitrary"), vmem_limit_bytes=VMEM_LIMIT),
        name="norm_matmul_small" if has_small else "norm_matmul",
    )(*args)


CUMSUM_BLOCK = 256
BIAS_TERMS = 3


def _bias_lane(head):
    return jnp.where(head % HEADS_PER_BLOCK == 0, HEAD_DIM, 0)


def _fox_bias_kernel(f_ref, b_ref, o_ref):
    t = f_ref.shape[1]
    row = lax.broadcasted_iota(jnp.int32, (CUMSUM_BLOCK, CUMSUM_BLOCK), 0)
    col = lax.broadcasted_iota(jnp.int32, (CUMSUM_BLOCK, CUMSUM_BLOCK), 1)
    tri = (row >= col).astype(BF16)
    head = lax.broadcasted_iota(jnp.int32, (LANES, HEAD_W), 0)
    lane = lax.broadcasted_iota(jnp.int32, (LANES, HEAD_W), 1)
    at_base = ((head < HEADS) & (lane // LANES == head // HEADS_PER_BLOCK))
    place = [(at_base & (lane % LANES == _bias_lane(head) + j)).astype(BF16)
             for j in range(BIAS_TERMS)]
    carry = jnp.zeros((1, LANES), F32)
    for blk in range(t // CUMSUM_BLOCK):
        sl = slice(blk * CUMSUM_BLOCK, (blk + 1) * CUMSUM_BLOCK)
        log_f = jax.nn.log_sigmoid(f_ref[0, sl, :] + b_ref[...])
        c = _dot_exact_lhs(tri, log_f) + carry
        carry = c[CUMSUM_BLOCK - 1:CUMSUM_BLOCK, :]
        terms = _split(c * (-(HEAD_DIM ** 0.5)), BIAS_TERMS)
        o_ref[0, sl, :] = sum(_dot(x, e) for x, e in zip(terms, place)).astype(o_ref.dtype)


def _fox_bias(small, f_bias_padded, batch, seq):
    return pl.pallas_call(
        _fox_bias_kernel,
        grid=(batch,),
        in_specs=[
            pl.BlockSpec((1, seq, LANES), lambda b: (b, 0, 1)),
            pl.BlockSpec((1, LANES), lambda b: (0, 0)),
        ],
        out_specs=pl.BlockSpec((1, seq, HEAD_W), lambda b: (b, 0, 0)),
        out_shape=jax.ShapeDtypeStruct((batch, seq, HEAD_W), BF16),
        compiler_params=pltpu.CompilerParams(
            dimension_semantics=("parallel",), vmem_limit_bytes=VMEM_LIMIT),
        name="fox_bias",
    )(small, f_bias_padded)


MASK_VALUE = -1e30
LOG2E = 1.4426950408889634


VT_ROWS = HEAD_DIM + 16


def _fox_attn_kernel(q_ref, k_ref, v_ref, bias_ref, g_ref, o_ref,
                     kaug_scr, vt_scr, m_scr, acc_scr, *, blk):
    qi = pl.program_id(2)
    seq = k_ref.shape[1]

    @pl.when(qi == 0)
    def _():
        lane = lax.broadcasted_iota(jnp.int32, (seq, LANES), 1)
        k = k_ref[0]
        bias = bias_ref[0]
        kaug_scr[0] = jnp.where(lane < HEAD_DIM, k, bias)
        kaug_scr[1] = jnp.where(lane >= HEAD_DIM, k, bias)
        ones_row = (lax.broadcasted_iota(jnp.int32, (VT_ROWS - HEAD_DIM, blk), 0) == 0).astype(BF16)
        for j in range(seq // blk):
            v_t = v_ref[0, j * blk:(j + 1) * blk, :].astype(F32).T.astype(BF16)
            for h in range(HEADS_PER_BLOCK):
                vt_scr[j, h] = jnp.concatenate(
                    [v_t[h * HEAD_DIM:(h + 1) * HEAD_DIM], ones_row], axis=0)

    m_scr[...] = jnp.full(m_scr.shape, MASK_VALUE, F32)
    acc_scr[...] = jnp.zeros(acc_scr.shape, F32)

    lane = lax.broadcasted_iota(jnp.int32, (blk, LANES), 1)
    q = q_ref[0]
    ones0 = ((lane >= HEAD_DIM) & (lane < HEAD_DIM + BIAS_TERMS)).astype(BF16)
    ones1 = (lane < BIAS_TERMS).astype(BF16)
    q_aug = jnp.stack([jnp.where(lane < HEAD_DIM, q, ones0), jnp.where(lane >= HEAD_DIM, q, ones1)])
    kappa = (HEAD_DIM ** -0.5) * LOG2E

    def block(j, masked):
        k_blk = kaug_scr[:, pl.ds(pl.multiple_of(j * blk, blk), blk), :]
        s_t = _bdot_nt(k_blk, q_aug)
        if masked:
            key = lax.broadcasted_iota(jnp.int32, (blk, blk), 0)
            qry = lax.broadcasted_iota(jnp.int32, (blk, blk), 1)
            s_t = jnp.where(key <= qry, s_t, MASK_VALUE)
        m_prev = m_scr[...]
        m_new = jnp.maximum(m_prev, jnp.max(s_t, axis=1, keepdims=True))
        alpha = jnp.exp2((m_prev - m_new) * kappa)
        p_t = jnp.exp2((s_t - m_new) * kappa).astype(BF16)
        acc_scr[...] = alpha * acc_scr[...] + _bdot(vt_scr[j], p_t)
        m_scr[...] = m_new

    def body(j, carry):
        block(j, masked=False)
        return carry

    lax.fori_loop(0, qi, body, 0)
    block(qi, masked=True)

    acc = acc_scr[...]
    o_t = acc[:, :HEAD_DIM, :] * (1.0 / acc[:, HEAD_DIM:HEAD_DIM + 1, :])
    o = o_t.reshape(LANES, blk).T
    o_ref[0] = (o * _silu(g_ref[0].astype(F32))).astype(o_ref.dtype)


def _fox_attention(main, bias, batch, seq, *, blk=512):
    nblk = HEAD_W // LANES
    nq = seq // blk
    return pl.pallas_call(
        functools.partial(_fox_attn_kernel, blk=blk),
        grid=(batch, nblk, nq),
        in_specs=[
            pl.BlockSpec((1, blk, LANES), lambda b, hp, qi: (b, qi, hp)),
            pl.BlockSpec((1, seq, LANES), lambda b, hp, qi: (b, 0, nblk + hp)),
            pl.BlockSpec((1, seq, LANES), lambda b, hp, qi: (b, 0, 2 * nblk + hp)),
            pl.BlockSpec((1, seq, LANES), lambda b, hp, qi: (b, 0, hp)),
            pl.BlockSpec((1, blk, LANES), lambda b, hp, qi: (b, qi, 3 * nblk + hp)),
        ],
        out_specs=pl.BlockSpec((1, blk, LANES), lambda b, hp, qi: (b, qi, hp)),
        out_shape=jax.ShapeDtypeStruct((batch, seq, HEAD_W), BF16),
        scratch_shapes=[
            pltpu.VMEM((HEADS_PER_BLOCK, seq, LANES), BF16),
            pltpu.VMEM((seq // blk, HEADS_PER_BLOCK, VT_ROWS, blk), BF16),
            pltpu.VMEM((HEADS_PER_BLOCK, 1, blk), F32),
            pltpu.VMEM((HEADS_PER_BLOCK, VT_ROWS, blk), F32),
        ],
        compiler_params=pltpu.CompilerParams(
            dimension_semantics=("parallel", "parallel", "arbitrary"),
            vmem_limit_bytes=VMEM_LIMIT),
        name="fox_attention",
    )(main, main, main, bias, main)


def _shift_lerp(x, prev_row, mu):
    rolled = pltpu.roll(x, 1, 0)
    first = lax.broadcasted_iota(jnp.int32, x.shape, 0) == 0
    x_prev = jnp.where(first, prev_row, rolled)
    return x + (x_prev - x) * mu


def _head_sum(x, terms):
    row = lax.broadcasted_iota(jnp.int32, (LANES, LANES), 0)
    col = lax.broadcasted_iota(jnp.int32, (LANES, LANES), 1)
    same_head = (row // HEAD_DIM == col // HEAD_DIM).astype(BF16)
    return sum(_dot(part, same_head) for part in _split(x, terms))


def _stack_heads(x):
    lane = lax.broadcasted_iota(jnp.int32, x.shape, x.ndim - 1)
    return jnp.concatenate(
        [jnp.where(lane < HEAD_DIM, x, 0.0), jnp.where(lane >= HEAD_DIM, x, 0.0)], axis=x.ndim - 2)


def _bdot(a, b):
    return lax.dot_general(a, b, (((2,), (1,)), ((0,), (0,))), preferred_element_type=F32)


def _bdot_nt(a, b):
    return lax.dot_general(a, b, (((2,), (2,)), ((0,), (0,))), preferred_element_type=F32)


def _bdot_tn(a, b):
    return lax.dot_general(a, b, (((1,), (1,)), ((0,), (0,))), preferred_element_type=F32)


def _unit_lower_inverse(l_strict, order):
    n = l_strict.shape[-1]
    eye = (lax.broadcasted_iota(jnp.int32, (n, n), 0)
           == lax.broadcasted_iota(jnp.int32, (n, n), 1)).astype(F32)
    inv = eye + l_strict
    power = l_strict.astype(BF16)
    power = _bdot(power, power).astype(BF16)
    last = order.bit_length() - 2
    for k in range(1, last):
        both = _bdot(jnp.concatenate([inv.astype(BF16), power], axis=1), power)
        inv = inv + both[:, :n]
        power = both[:, n:].astype(BF16)
    return inv + _bdot(inv.astype(BF16), power)


def _rwkv_kernel(r_ref, k_ref, v_ref, g_ref, wa_ref, mu_ref, par_ref, w2_ref, a2_ref,
                 o_ref, state_scr, prev_scr, *, tc):
    ti = pl.program_id(2)

    @pl.when(ti == 0)
    def _():
        state_scr[...] = jnp.zeros(state_scr.shape, F32)
        prev_scr[...] = jnp.zeros(prev_scr.shape, F32)

    r_in = r_ref[0].astype(F32)
    k_in = k_ref[0].astype(F32)
    v_in = v_ref[0].astype(F32)
    wa_in = wa_ref[0]
    r = _shift_lerp(r_in, prev_scr[0:1, :], mu_ref[0:1, :])
    k = _shift_lerp(k_in, prev_scr[1:2, :], mu_ref[1:2, :])
    v = _shift_lerp(v_in, prev_scr[2:3, :], mu_ref[2:3, :])
    wa = _shift_lerp(wa_in, prev_scr[3:4, :], mu_ref[3:4, :])
    prev_scr[0:1, :] = r_in[tc - 1:tc, :]
    prev_scr[1:2, :] = k_in[tc - 1:tc, :]
    prev_scr[2:3, :] = v_in[tc - 1:tc, :]
    prev_scr[3:4, :] = wa_in[tc - 1:tc, :]

    w0, a0, k_k, k_a, r_k, ln_g, ln_b = (par_ref[i:i + 1, :] for i in range(7))
    w_lo = jnp.tanh(wa[:, :LORA]).astype(BF16)
    a_lo = wa[:, LORA:].astype(BF16)
    w_raw = -jax.nn.softplus(-(w0 + _dot(w_lo, w2_ref[...]))) - 0.5
    logw = -jnp.exp(w_raw)
    a = jax.nn.sigmoid(a0 + _dot(a_lo, a2_ref[...]))
    kk = k * k_k
    kk = kk / jnp.maximum(jnp.sqrt(_head_sum(kk * kk, 2)), 1e-12)
    k = k * (1.0 + (a - 1.0) * k_a)

    row = lax.broadcasted_iota(jnp.int32, (LANES, LANES), 0)
    col = lax.broadcasted_iota(jnp.int32, (LANES, LANES), 1)
    tri_blk = ((row >= col) & (row // RWKV_CHUNK == col // RWKV_CHUNK)).astype(BF16)
    cum = jnp.concatenate(
        [_dot_exact_lhs(tri_blk, logw[i:i + LANES]) for i in range(0, tc, LANES)], axis=0)

    c = RWKV_CHUNK
    nc = tc // c
    n2 = HEADS_PER_BLOCK * c
    srow = lax.broadcasted_iota(jnp.int32, (n2, n2), 0)
    scol = lax.broadcasted_iota(jnp.int32, (n2, n2), 1)
    lower_incl = srow >= scol
    lower_strict = srow > scol

    def chunked(x):
        return x.reshape(nc, c, LANES)

    cum3 = chunked(cum)
    cum_end = cum3[:, c - 1:c, :]
    e_neg = jnp.exp(-cum3)
    e_end = jnp.exp(cum_end - cum3)
    akk = chunked(a * kk)
    k3 = chunked(k)
    r_s = _stack_heads(chunked(r) * jnp.exp(cum3))
    p_s = _stack_heads(-chunked(kk) * jnp.exp(cum3 - chunked(logw))).astype(BF16)
    z_s = _stack_heads(akk * e_neg).astype(BF16)
    k_s = _stack_heads(k3 * e_neg).astype(BF16)
    zbar_s = _stack_heads(akk * e_end).astype(BF16)
    kbar_s = _stack_heads(k3 * e_end).astype(BF16)
    v_s = _stack_heads(chunked(v)).astype(BF16)
    g_end = jnp.exp(cum_end)
    pr = jnp.concatenate([p_s, r_s.astype(BF16)], axis=1)
    zk = jnp.concatenate([z_s, k_s], axis=1)
    amat = _bdot_nt(pr, zk)
    l_pz = jnp.where(lower_strict, amat[:, :n2, :n2], 0.0)
    l_pk = jnp.where(lower_strict, amat[:, :n2, n2:], 0.0).astype(BF16)
    l_rz = jnp.where(lower_incl, amat[:, n2:, :n2], 0.0).astype(BF16)
    l_rk = jnp.where(lower_incl, amat[:, n2:, n2:], 0.0).astype(BF16)
    inv = _unit_lower_inverse(l_pz, c).astype(BF16)
    lv = _bdot(l_pk, v_s).astype(BF16)
    tp_ui = _bdot(inv, jnp.concatenate([p_s, lv], axis=2)).astype(BF16)
    rz = _bdot(l_rz, tp_ui)
    qeff = (r_s + rz[:, :, :LANES]).astype(BF16)
    yi = rz[:, :, LANES:] + _bdot(l_rk, v_s)
    mz = _bdot_tn(tp_ui, zbar_s)
    m_low = mz[:, :LANES].astype(BF16)
    b_mat = mz[:, LANES:] + _bdot_tn(v_s, kbar_s)

    s = state_scr[...]
    ys = []
    for ci in range(nc):
        s_b = s.astype(BF16)
        y_s = lax.dot_general(qeff[ci], s_b, NT_DIMS, preferred_element_type=F32) + yi[ci]
        ys.append(y_s[:c] + y_s[c:])
        s = s * g_end[ci] + _dot(s_b, m_low[ci]) + b_mat[ci]
    state_scr[...] = s

    y = jnp.concatenate(ys, axis=0)
    mean = _head_sum(y, 2) * (1.0 / HEAD_DIM)
    yc = y - mean
    var = _head_sum(yc * yc, 2) * (1.0 / HEAD_DIM)
    y = yc * lax.rsqrt(var + RWKV_GN_EPS) * ln_g + ln_b
    y = y + _head_sum(r * k * r_k, 2) * v
    o_ref[0] = (y * _silu(g_ref[0].astype(F32))).astype(o_ref.dtype)


def _rwkv(main, small, mu_rows, par_rows, w2, a2, batch, seq, *, tc=512):
    nblk = HEAD_W // LANES
    return pl.pallas_call(
        functools.partial(_rwkv_kernel, tc=tc),
        grid=(batch, nblk, seq // tc),
        in_specs=[
            pl.BlockSpec((1, tc, LANES), lambda b, hp, ti: (b, ti, 4 * nblk + hp)),
            pl.BlockSpec((1, tc, LANES), lambda b, hp, ti: (b, ti, 5 * nblk + hp)),
            pl.BlockSpec((1, tc, LANES), lambda b, hp, ti: (b, ti, 6 * nblk + hp)),
            pl.BlockSpec((1, tc, LANES), lambda b, hp, ti: (b, ti, 7 * nblk + hp)),
            pl.BlockSpec((1, tc, LANES), lambda b, hp, ti: (b, ti, 0)),
            pl.BlockSpec((None, 8, LANES), lambda b, hp, ti: (hp, 0, 0)),
            pl.BlockSpec((None, 8, LANES), lambda b, hp, ti: (hp, 0, 0)),
            pl.BlockSpec((LORA, LANES), lambda b, hp, ti: (0, hp)),
            pl.BlockSpec((LORA, LANES), lambda b, hp, ti: (0, hp)),
        ],
        out_specs=pl.BlockSpec((1, tc, LANES), lambda b, hp, ti: (b, ti, hp)),
        out_shape=jax.ShapeDtypeStruct((batch, seq, HEAD_W), BF16),
        scratch_shapes=[
            pltpu.VMEM((LANES, LANES), F32),
            pltpu.VMEM((8, LANES), F32),
        ],
        compiler_params=pltpu.CompilerParams(
            dimension_semantics=("parallel", "parallel", "arbitrary"),
            vmem_limit_bytes=VMEM_LIMIT),
        name="rwkv7",
    )(main, main, main, main, small, mu_rows, par_rows, w2, a2)


def _sgu_kernel(u_ref, v_ref, g_ref, lng_ref, lnb_ref, ws_ref, bs_ref, o_ref, *, rows):
    c = GMLP_CHUNK
    causal = (lax.broadcasted_iota(jnp.int32, (c, c), 0)
              >= lax.broadcasted_iota(jnp.int32, (c, c), 1))
    v = _gelu(v_ref[...].astype(F32))
    mu = jnp.mean(v, axis=-1, keepdims=True)
    vc = v - mu
    var = jnp.mean(vc * vc, axis=-1, keepdims=True)
    vn = (vc * lax.rsqrt(var + LN_EPS) * lng_ref[...] + lnb_ref[...]).astype(BF16)
    for grp in range(GMLP_GROUPS):
        cs = slice(grp * c, (grp + 1) * c)
        w = jnp.where(causal, ws_ref[grp], 0.0).astype(BF16)
        bias = bs_ref[:, grp:grp + 1]
        for ch in range(rows // c):
            rs = slice(ch * c, (ch + 1) * c)
            mixed = _dot(w, vn[rs, cs]) + bias
            u = _gelu(u_ref[rs, cs].astype(F32))
            o_ref[rs, cs] = (u * mixed * _silu(g_ref[rs, cs].astype(F32))).astype(o_ref.dtype)


def _spatial_gating(proj, ln_g, ln_b, w_s, b_s_t, *, rows=256):
    m = proj.shape[0]
    d = D_MODEL
    return pl.pallas_call(
        functools.partial(_sgu_kernel, rows=rows),
        grid=(m // rows,),
        in_specs=[
            pl.BlockSpec((rows, d), lambda i: (i, 0)),
            pl.BlockSpec((rows, d), lambda i: (i, 1)),
            pl.BlockSpec((rows, d), lambda i: (i, 2)),
            pl.BlockSpec((1, d), lambda i: (0, 0)),
            pl.BlockSpec((1, d), lambda i: (0, 0)),
            pl.BlockSpec((GMLP_GROUPS, GMLP_CHUNK, GMLP_CHUNK), lambda i: (0, 0, 0)),
            pl.BlockSpec((GMLP_CHUNK, GMLP_GROUPS), lambda i: (0, 0)),
        ],
        out_specs=pl.BlockSpec((rows, d), lambda i: (i, 0)),
        out_shape=jax.ShapeDtypeStruct((m, d), BF16),
        compiler_params=pltpu.CompilerParams(
            dimension_semantics=("parallel",), vmem_limit_bytes=VMEM_LIMIT),
        name="spatial_gating",
    )(proj, proj, proj, ln_g, ln_b, w_s, b_s_t)


def _tail_kernel(a0_ref, a1_ref, h_ref, p_ref, w0_ref, w1_ref, gp_ref, wp_ref, wg_ref, o_ref):
    y = _dot(a0_ref[...], w0_ref[...]) + _dot(a1_ref[...], w1_ref[...])
    ms = jnp.mean(y * y, axis=-1, keepdims=True)
    h1 = h_ref[...] + y * lax.rsqrt(ms + RMS_EPS) * gp_ref[...]
    gate = _dot(h1.astype(BF16), wg_ref[...])
    pp = _dot(p_ref[...].astype(BF16), wp_ref[...])
    o_ref[...] = h1 + pp * jax.nn.sigmoid(gate)


def _tail(act0, blk0, act1, blk1, h, p, w_out, g_post, w_proj, w_gate, *, tm=256):
    m, d = h.shape
    half = w_out.shape[0] // 2
    return pl.pallas_call(
        _tail_kernel,
        grid=(m // tm,),
        in_specs=[
            pl.BlockSpec((tm, half), lambda i: (i, blk0)),
            pl.BlockSpec((tm, half), lambda i: (i, blk1)),
            pl.BlockSpec((tm, d), lambda i: (i, 0)),
            pl.BlockSpec((tm, PLE_DIM), lambda i: (i, 0)),
            pl.BlockSpec((half, d), lambda i: (0, 0)),
            pl.BlockSpec((half, d), lambda i: (1, 0)),
            pl.BlockSpec((1, d), lambda i: (0, 0)),
            pl.BlockSpec((PLE_DIM, d), lambda i: (0, 0)),
            pl.BlockSpec((d, d), lambda i: (0, 0)),
        ],
        out_specs=pl.BlockSpec((tm, d), lambda i: (i, 0)),
        out_shape=jax.ShapeDtypeStruct((m, d), F32),
        compiler_params=pltpu.CompilerParams(
            dimension_semantics=("parallel",), vmem_limit_bytes=VMEM_LIMIT),
        name="tail",
    )(act0, act1, h, p, w_out, w_out, g_post, w_proj, w_gate)


def _per_block_rows(vecs):
    nblk = HEAD_W // LANES
    rows = [v.reshape(nblk, 1, LANES) for v in vecs]
    rows += [jnp.zeros((nblk, 1, LANES), F32)] * (8 - len(rows))
    return jnp.concatenate(rows, axis=1)


def kernel(x, p, norm_pre, norm_post, ab_w_in, fox_f_bias, rwkv_mu, rwkv_w0, rwkv_w2, rwkv_a0, rwkv_a2, rwkv_k_k, rwkv_k_a, rwkv_r_k, rwkv_ln_g, rwkv_ln_b, ab_w_out, c_w_in, c_ln_g, c_ln_b, c_w_s, c_b_s, c_w_out, ple_w_proj, ple_w_gate):
    batch, seq, d = x.shape
    m = batch * seq
    w = HEAD_W
    h = x.reshape(m, d)

    w_in = ab_w_in[0]
    o_f = 3 * w
    o_ga = o_f + HEADS
    o_sh = o_ga + w
    o_lo = o_sh + 3 * w
    o_gb = o_lo + 2 * LORA
    w_main = jnp.concatenate(
        [w_in[:, :o_f], w_in[:, o_ga:o_sh], w_in[:, o_sh:o_lo], w_in[:, o_gb:]], axis=1).astype(BF16)
    w_small = jnp.concatenate(
        [w_in[:, o_lo:o_gb], w_in[:, o_f:o_ga], jnp.zeros((d, LANES - HEADS), F32)], axis=1).astype(BF16)
    main, small = _norm_matmul(h, norm_pre[0:1], w_main, w_small)
    main3 = main.reshape(batch, seq, 8 * w)
    small3 = small.reshape(batch, seq, 2 * LANES)

    f_bias = jnp.concatenate([fox_f_bias[0], jnp.zeros((LANES - HEADS,), F32)]).reshape(1, LANES)
    fox_bias = _fox_bias(small3, f_bias, batch, seq)
    act_a = _fox_attention(main3, fox_bias, batch, seq)

    mu = rwkv_mu[0]
    mu_rows = _per_block_rows([mu[:w], mu[w:2 * w], mu[2 * w:3 * w]])
    mu_rows = mu_rows.at[:, 3, :].set(jnp.broadcast_to(mu[3 * w:], (w // LANES, LANES)))
    par_rows = _per_block_rows([rwkv_w0[0], rwkv_a0[0], rwkv_k_k[0], rwkv_k_a[0],
                                rwkv_r_k[0].reshape(w), rwkv_ln_g[0], rwkv_ln_b[0]])
    act_b = _rwkv(main3, small3, mu_rows, par_rows, rwkv_w2[0].astype(BF16),
                  rwkv_a2[0].astype(BF16), batch, seq)

    h = _tail(act_a.reshape(m, w), 0, act_b.reshape(m, w), 0, h, p[0].reshape(m, PLE_DIM),
              ab_w_out[0].astype(BF16), norm_post[0:1], ple_w_proj[0].astype(BF16),
              ple_w_gate[0].astype(BF16))

    proj = _norm_matmul(h, norm_pre[1:2], c_w_in[0].astype(BF16))[0]
    act_c = _spatial_gating(proj, c_ln_g[0:1], c_ln_b[0:1], c_w_s[0], c_b_s[0].T)
    h = _tail(act_c, 0, act_c, 1, h, p[1].reshape(m, PLE_DIM),
              c_w_out[0].astype(BF16), norm_post[1:2], ple_w_proj[1].astype(BF16),
              ple_w_gate[1].astype(BF16))
    return h.reshape(batch, seq, d)
```

```python
import functools

import jax
import jax.numpy as jnp
from jax import lax
from jax.experimental import pallas as pl
from jax.experimental.pallas import tpu as pltpu

F32 = jnp.float32
BF16 = jnp.bfloat16

D_MODEL = 2048
PLE_DIM = 256
HEADS = 16
HEAD_DIM = 64
HEAD_W = HEADS * HEAD_DIM
LORA = 64
GMLP_GROUPS = 16
GMLP_CHUNK = 128
RMS_EPS = 1e-6
LN_EPS = 1e-5
RWKV_GN_EPS = 64e-5
LANES = 128
HEADS_PER_BLOCK = LANES // HEAD_DIM
RWKV_CHUNK = 64
VMEM_LIMIT = 56 * 1024 * 1024

NT_DIMS = (((1,), (1,)), ((), ()))
TN_DIMS = (((0,), (0,)), ((), ()))


def _dot(a, b):
    return jnp.dot(a, b, preferred_element_type=F32)


def _split(x, terms):
    parts = []
    for _ in range(terms - 1):
        part = x.astype(BF16)
        parts.append(part)
        x = x - part.astype(F32)
    parts.append(x.astype(BF16))
    return parts


def _dot_exact_lhs(a_bf16, x, terms=3):
    return sum(_dot(a_bf16, part) for part in _split(x, terms))


def _resident(block_shape, index_map):
    return pl.BlockSpec(block_shape, index_map, pipeline_mode=pl.Buffered(1))


def _silu(x):
    return x * jax.nn.sigmoid(x)


def _gelu(x):
    return 0.5 * x * (1.0 + lax.erf(x * (0.5 ** 0.5)))


def _norm_mm_kernel(x_ref, g_ref, w_ref, *rest, has_small):
    if has_small:
        ws_ref, o_ref, os_ref, xn_ref = rest
    else:
        o_ref, xn_ref = rest

    @pl.when(pl.program_id(1) == 0)
    def _():
        x = x_ref[...]
        ms = jnp.mean(x * x, axis=-1, keepdims=True)
        xn = (x * lax.rsqrt(ms + RMS_EPS) * g_ref[...]).astype(BF16)
        xn_ref[...] = xn
        if has_small:
            os_ref[...] = _dot(xn, ws_ref[...])

    o_ref[...] = _dot(xn_ref[...], w_ref[...]).astype(o_ref.dtype)


def _norm_matmul(x, g, w, w_small=None, *, tm=1024, tn=2048):
    m, d = x.shape
    n = w.shape[1]
    has_small = w_small is not None
    in_specs = [
        pl.BlockSpec((tm, d), lambda i, j: (i, 0)),
        _resident((1, d), lambda i, j: (0, 0)),
        pl.BlockSpec((d, tn), lambda i, j: (0, j)),
    ]
    out_specs = [pl.BlockSpec((tm, tn), lambda i, j: (i, j))]
    out_shape = [jax.ShapeDtypeStruct((m, n), BF16)]
    args = [x, g, w]
    if has_small:
        ns = w_small.shape[1]
        in_specs.append(_resident((d, ns), lambda i, j: (0, 0)))
        out_specs.append(pl.BlockSpec((tm, ns), lambda i, j: (i, 0)))
        out_shape.append(jax.ShapeDtypeStruct((m, ns), F32))
        args.append(w_small)
    return pl.pallas_call(
        functools.partial(_norm_mm_kernel, has_small=has_small),
        grid=(m // tm, n // tn),
        in_specs=in_specs,
        out_specs=out_specs,
        out_shape=out_shape,
        scratch_shapes=[pltpu.VMEM((tm, d), BF16)],
        compiler_params=pltpu.CompilerParams(
            dimension_semantics=("parallel", "arbitrary"), vmem_limit_bytes=VMEM_LIMIT),
        name="norm_matmul_small" if has_small else "norm_matmul",
    )(*args)


CUMSUM_BLOCK = 256
BIAS_TERMS = 3


def _bias_lane(head):
    return jnp.where(head % HEADS_PER_BLOCK == 0, HEAD_DIM, 0)


def _fox_bias_kernel(f_ref, b_ref, o_ref):
    t = f_ref.shape[1]
    row = lax.broadcasted_iota(jnp.int32, (CUMSUM_BLOCK, CUMSUM_BLOCK), 0)
    col = lax.broadcasted_iota(jnp.int32, (CUMSUM_BLOCK, CUMSUM_BLOCK), 1)
    tri = (row >= col).astype(BF16)
    head = lax.broadcasted_iota(jnp.int32, (LANES, HEAD_W), 0)
    lane = lax.broadcasted_iota(jnp.int32, (LANES, HEAD_W), 1)
    at_base = ((head < HEADS) & (lane // LANES == head // HEADS_PER_BLOCK))
    place = [(at_base & (lane % LANES == _bias_lane(head) + j)).astype(BF16)
             for j in range(BIAS_TERMS)]
    carry = jnp.zeros((1, LANES), F32)
    for blk in range(t // CUMSUM_BLOCK):
        sl = slice(blk * CUMSUM_BLOCK, (blk + 1) * CUMSUM_BLOCK)
        log_f = jax.nn.log_sigmoid(f_ref[0, sl, :] + b_ref[...])
        c = _dot_exact_lhs(tri, log_f) + carry
        carry = c[CUMSUM_BLOCK - 1:CUMSUM_BLOCK, :]
        terms = _split(c * (-(HEAD_DIM ** 0.5)), BIAS_TERMS)
        o_ref[0, sl, :] = sum(_dot(x, e) for x, e in zip(terms, place)).astype(o_ref.dtype)


def _fox_bias(small, f_bias_padded, batch, seq):
    return pl.pallas_call(
        _fox_bias_kernel,
        grid=(batch,),
        in_specs=[
            pl.BlockSpec((1, seq, LANES), lambda b: (b, 0, 1)),
            pl.BlockSpec((1, LANES), lambda b: (0, 0)),
        ],
        out_specs=pl.BlockSpec((1, seq, HEAD_W), lambda b: (b, 0, 0)),
        out_shape=jax.ShapeDtypeStruct((batch, seq, HEAD_W), BF16),
        compiler_params=pltpu.CompilerParams(
            dimension_semantics=("parallel",), vmem_limit_bytes=VMEM_LIMIT),
        name="fox_bias",
    )(small, f_bias_padded)


MASK_VALUE = -1e30
LOG2E = 1.4426950408889634


VT_ROWS = HEAD_DIM + 16


def _fox_attn_kernel(q_ref, k_ref, v_ref, bias_ref, g_ref, o_ref,
                     kaug_scr, vt_scr, *, blk):
    qi = pl.program_id(2)
    seq = k_ref.shape[1]

    @pl.when(qi == 0)
    def _():
        lane = lax.broadcasted_iota(jnp.int32, (seq, LANES), 1)
        k = k_ref[0]
        bias = bias_ref[0]
        kaug_scr[0] = jnp.where(lane < HEAD_DIM, k, bias)
        kaug_scr[1] = jnp.where(lane >= HEAD_DIM, k, bias)
        ones_row = (lax.broadcasted_iota(jnp.int32, (VT_ROWS - HEAD_DIM, blk), 0) == 0).astype(BF16)
        for j in range(seq // blk):
            v_t = v_ref[0, j * blk:(j + 1) * blk, :].astype(F32).T.astype(BF16)
            for h in range(HEADS_PER_BLOCK):
                vt_scr[j, h] = jnp.concatenate(
                    [v_t[h * HEAD_DIM:(h + 1) * HEAD_DIM], ones_row], axis=0)

    lane = lax.broadcasted_iota(jnp.int32, (blk, LANES), 1)
    q = q_ref[0]
    ones0 = ((lane >= HEAD_DIM) & (lane < HEAD_DIM + BIAS_TERMS)).astype(BF16)
    ones1 = (lane < BIAS_TERMS).astype(BF16)
    q_aug = jnp.stack([jnp.where(lane < HEAD_DIM, q, ones0), jnp.where(lane >= HEAD_DIM, q, ones1)])
    kappa = (HEAD_DIM ** -0.5) * LOG2E

    def scores(j):
        return _bdot_nt(kaug_scr[:, j * blk:(j + 1) * blk, :], q_aug)

    def softmax_step(s_t, m_prev, masked):
        if masked:
            key = lax.broadcasted_iota(jnp.int32, (blk, blk), 0)
            qry = lax.broadcasted_iota(jnp.int32, (blk, blk), 1)
            s_t = jnp.where(key <= qry, s_t, MASK_VALUE)
        m_new = jnp.maximum(m_prev, jnp.max(s_t, axis=1, keepdims=True))
        alpha = jnp.exp2((m_prev - m_new) * kappa)
        return jnp.exp2((s_t - m_new) * kappa).astype(BF16), alpha, m_new

    def attend(last):
        m = jnp.full((HEADS_PER_BLOCK, 1, blk), MASK_VALUE, F32)
        acc = jnp.zeros((HEADS_PER_BLOCK, VT_ROWS, blk), F32)
        s_next = scores(0)
        pending = None
        for j in range(last + 1):
            s_cur = s_next
            if j < last:
                s_next = scores(j + 1)
            p_t, alpha, m = softmax_step(s_cur, m, masked=(j == last))
            if pending is not None:
                acc = pending[2] * acc + _bdot(vt_scr[pending[0]], pending[1])
            pending = (j, p_t, alpha)
        acc = pending[2] * acc + _bdot(vt_scr[pending[0]], pending[1])
        o_t = acc[:, :HEAD_DIM, :] * (1.0 / acc[:, HEAD_DIM:HEAD_DIM + 1, :])
        o = o_t.reshape(LANES, blk).T
        o_ref[0] = (o * _silu(g_ref[0].astype(F32))).astype(o_ref.dtype)

    for last in range(seq // blk):
        pl.when(qi == last)(functools.partial(attend, last))


def _fox_attention(main, bias, batch, seq, *, blk=512):
    nblk = HEAD_W // LANES
    nq = seq // blk
    return pl.pallas_call(
        functools.partial(_fox_attn_kernel, blk=blk),
        grid=(batch, nblk, nq),
        in_specs=[
            pl.BlockSpec((1, blk, LANES), lambda b, hp, qi: (b, qi, hp)),
            pl.BlockSpec((1, seq, LANES), lambda b, hp, qi: (b, 0, nblk + hp)),
            pl.BlockSpec((1, seq, LANES), lambda b, hp, qi: (b, 0, 2 * nblk + hp)),
            pl.BlockSpec((1, seq, LANES), lambda b, hp, qi: (b, 0, hp)),
            pl.BlockSpec((1, blk, LANES), lambda b, hp, qi: (b, qi, 3 * nblk + hp)),
        ],
        out_specs=pl.BlockSpec((1, blk, LANES), lambda b, hp, qi: (b, qi, hp)),
        out_shape=jax.ShapeDtypeStruct((batch, seq, HEAD_W), BF16),
        scratch_shapes=[
            pltpu.VMEM((HEADS_PER_BLOCK, seq, LANES), BF16),
            pltpu.VMEM((seq // blk, HEADS_PER_BLOCK, VT_ROWS, blk), BF16),
        ],
        compiler_params=pltpu.CompilerParams(
            dimension_semantics=("parallel", "parallel", "arbitrary"),
            vmem_limit_bytes=VMEM_LIMIT),
        name="fox_attention",
    )(main, main, main, bias, main)


def _shift_lerp(x, prev_row, mu):
    rolled = pltpu.roll(x, 1, 0)
    first = lax.broadcasted_iota(jnp.int32, x.shape, 0) == 0
    x_prev = jnp.where(first, prev_row, rolled)
    return x + (x_prev - x) * mu


def _head_sum(x, terms):
    row = lax.broadcasted_iota(jnp.int32, (LANES, LANES), 0)
    col = lax.broadcasted_iota(jnp.int32, (LANES, LANES), 1)
    same_head = (row // HEAD_DIM == col // HEAD_DIM).astype(BF16)
    return sum(_dot(part, same_head) for part in _split(x, terms))


def _stack_heads(x):
    lane = lax.broadcasted_iota(jnp.int32, x.shape, x.ndim - 1)
    return jnp.concatenate(
        [jnp.where(lane < HEAD_DIM, x, 0.0), jnp.where(lane >= HEAD_DIM, x, 0.0)], axis=x.ndim - 2)


def _bdot(a, b):
    return lax.dot_general(a, b, (((2,), (1,)), ((0,), (0,))), preferred_element_type=F32)


def _bdot_nt(a, b):
    return lax.dot_general(a, b, (((2,), (2,)), ((0,), (0,))), preferred_element_type=F32)


def _bdot_tn(a, b):
    return lax.dot_general(a, b, (((1,), (1,)), ((0,), (0,))), preferred_element_type=F32)


def _unit_lower_inverse(l_strict, order):
    n = l_strict.shape[-1]
    eye = (lax.broadcasted_iota(jnp.int32, (n, n), 0)
           == lax.broadcasted_iota(jnp.int32, (n, n), 1)).astype(F32)
    inv = eye + l_strict
    power = l_strict.astype(BF16)
    power = _bdot(power, power).astype(BF16)
    last = order.bit_length() - 2
    for k in range(1, last):
        both = _bdot(jnp.concatenate([inv.astype(BF16), power], axis=1), power)
        inv = inv + both[:, :n]
        power = both[:, n:].astype(BF16)
    return inv + _bdot(inv.astype(BF16), power)


def _rwkv_kernel(r_ref, k_ref, v_ref, g_ref, wa_ref, mu_ref, par_ref, w2_ref, a2_ref,
                 o_ref, state_scr, prev_scr, *, tc):
    ti = pl.program_id(2)

    @pl.when(ti == 0)
    def _():
        state_scr[...] = jnp.zeros(state_scr.shape, F32)
        prev_scr[...] = jnp.zeros(prev_scr.shape, F32)

    r_in = r_ref[0].astype(F32)
    k_in = k_ref[0].astype(F32)
    v_in = v_ref[0].astype(F32)
    wa_in = wa_ref[0]
    r = _shift_lerp(r_in, prev_scr[0:1, :], mu_ref[0:1, :])
    k = _shift_lerp(k_in, prev_scr[1:2, :], mu_ref[1:2, :])
    v = _shift_lerp(v_in, prev_scr[2:3, :], mu_ref[2:3, :])
    wa = _shift_lerp(wa_in, prev_scr[3:4, :], mu_ref[3:4, :])
    prev_scr[0:1, :] = r_in[tc - 1:tc, :]
    prev_scr[1:2, :] = k_in[tc - 1:tc, :]
    prev_scr[2:3, :] = v_in[tc - 1:tc, :]
    prev_scr[3:4, :] = wa_in[tc - 1:tc, :]

    w0, a0, k_k, k_a, r_k, ln_g, ln_b = (par_ref[i:i + 1, :] for i in range(7))
    w_lo = jnp.tanh(wa[:, :LORA]).astype(BF16)
    a_lo = wa[:, LORA:].astype(BF16)
    w_raw = -jax.nn.softplus(-(w0 + _dot(w_lo, w2_ref[...]))) - 0.5
    logw = -jnp.exp(w_raw)
    a = jax.nn.sigmoid(a0 + _dot(a_lo, a2_ref[...]))
    kk = k * k_k
    kk = kk / jnp.maximum(jnp.sqrt(_head_sum(kk * kk, 1)), 1e-12)
    k = k * (1.0 + (a - 1.0) * k_a)

    row = lax.broadcasted_iota(jnp.int32, (LANES, LANES), 0)
    col = lax.broadcasted_iota(jnp.int32, (LANES, LANES), 1)
    tri_blk = ((row >= col) & (row // RWKV_CHUNK == col // RWKV_CHUNK)).astype(BF16)
    cum = jnp.concatenate(
        [_dot_exact_lhs(tri_blk, logw[i:i + LANES], 2) for i in range(0, tc, LANES)], axis=0)

    c = RWKV_CHUNK
    nc = tc // c
    n2 = HEADS_PER_BLOCK * c
    srow = lax.broadcasted_iota(jnp.int32, (n2, n2), 0)
    scol = lax.broadcasted_iota(jnp.int32, (n2, n2), 1)
    lower_incl = srow >= scol
    lower_strict = srow > scol

    def chunked(x):
        return x.reshape(nc, c, LANES)

    cum3 = chunked(cum)
    cum_end = cum3[:, c - 1:c, :]
    e_neg = jnp.exp(-cum3)
    e_end = jnp.exp(cum_end - cum3)
    akk = chunked(a * kk)
    k3 = chunked(k)
    r_s = _stack_heads(chunked(r) * jnp.exp(cum3))
    p_s = _stack_heads(-chunked(kk) * jnp.exp(cum3 - chunked(logw))).astype(BF16)
    z_s = _stack_heads(akk * e_neg).astype(BF16)
    k_s = _stack_heads(k3 * e_neg).astype(BF16)
    zbar_s = _stack_heads(akk * e_end).astype(BF16)
    kbar_s = _stack_heads(k3 * e_end).astype(BF16)
    v_s = _stack_heads(chunked(v)).astype(BF16)
    g_end = jnp.exp(cum_end)
    pr = jnp.concatenate([p_s, r_s.astype(BF16)], axis=1)
    zk = jnp.concatenate([z_s, k_s], axis=1)
    amat = _bdot_nt(pr, zk)
    l_pz = jnp.where(lower_strict, amat[:, :n2, :n2], 0.0)
    l_pk = jnp.where(lower_strict, amat[:, :n2, n2:], 0.0).astype(BF16)
    l_rz = jnp.where(lower_incl, amat[:, n2:, :n2], 0.0).astype(BF16)
    l_rk = jnp.where(lower_incl, amat[:, n2:, n2:], 0.0).astype(BF16)
    inv = _unit_lower_inverse(l_pz, c).astype(BF16)
    lv = _bdot(l_pk, v_s).astype(BF16)
    tp_ui = _bdot(inv, jnp.concatenate([p_s, lv], axis=2)).astype(BF16)
    rz = _bdot(l_rz, tp_ui)
    qeff = (r_s + rz[:, :, :LANES]).astype(BF16)
    yi = rz[:, :, LANES:] + _bdot(l_rk, v_s)
    mz = _bdot_tn(tp_ui, zbar_s)
    m_low = mz[:, :LANES].astype(BF16)
    b_mat = mz[:, LANES:] + _bdot_tn(v_s, kbar_s)

    s = state_scr[...]
    ys = []
    for ci in range(nc):
        s_b = s.astype(BF16)
        y_s = lax.dot_general(qeff[ci], s_b, NT_DIMS, preferred_element_type=F32) + yi[ci]
        ys.append(y_s[:c] + y_s[c:])
        s = s * g_end[ci] + _dot(s_b, m_low[ci]) + b_mat[ci]
    state_scr[...] = s

    y = jnp.concatenate(ys, axis=0)
    mean = _head_sum(y, 2) * (1.0 / HEAD_DIM)
    yc = y - mean
    var = _head_sum(yc * yc, 1) * (1.0 / HEAD_DIM)
    y = yc * lax.rsqrt(var + RWKV_GN_EPS) * ln_g + ln_b
    y = y + _head_sum(r * k * r_k, 1) * v
    o_ref[0] = (y * _silu(g_ref[0].astype(F32))).astype(o_ref.dtype)


def _rwkv(main, small, mu_rows, par_rows, w2, a2, batch, seq, *, tc=512):
    nblk = HEAD_W // LANES
    return pl.pallas_call(
        functools.partial(_rwkv_kernel, tc=tc),
        grid=(batch, nblk, seq // tc),
        in_specs=[
            pl.BlockSpec((1, tc, LANES), lambda b, hp, ti: (b, ti, 4 * nblk + hp)),
            pl.BlockSpec((1, tc, LANES), lambda b, hp, ti: (b, ti, 5 * nblk + hp)),
            pl.BlockSpec((1, tc, LANES), lambda b, hp, ti: (b, ti, 6 * nblk + hp)),
            pl.BlockSpec((1, tc, LANES), lambda b, hp, ti: (b, ti, 7 * nblk + hp)),
            pl.BlockSpec((1, tc, LANES), lambda b, hp, ti: (b, ti, 0)),
            pl.BlockSpec((None, 8, LANES), lambda b, hp, ti: (hp, 0, 0)),
            pl.BlockSpec((None, 8, LANES), lambda b, hp, ti: (hp, 0, 0)),
            pl.BlockSpec((LORA, LANES), lambda b, hp, ti: (0, hp)),
            pl.BlockSpec((LORA, LANES), lambda b, hp, ti: (0, hp)),
        ],
        out_specs=pl.BlockSpec((1, tc, LANES), lambda b, hp, ti: (b, ti, hp)),
        out_shape=jax.ShapeDtypeStruct((batch, seq, HEAD_W), BF16),
        scratch_shapes=[
            pltpu.VMEM((LANES, LANES), F32),
            pltpu.VMEM((8, LANES), F32),
        ],
        compiler_params=pltpu.CompilerParams(
            dimension_semantics=("parallel", "parallel", "arbitrary"),
            vmem_limit_bytes=VMEM_LIMIT),
        name="rwkv7",
    )(main, main, main, main, small, mu_rows, par_rows, w2, a2)


def _sgu_kernel(u_ref, v_ref, g_ref, lng_ref, lnb_ref, ws_ref, bs_ref, o_ref, *, rows):
    c = GMLP_CHUNK
    causal = (lax.broadcasted_iota(jnp.int32, (c, c), 0)
              >= lax.broadcasted_iota(jnp.int32, (c, c), 1))
    v = _gelu(v_ref[...].astype(F32))
    mu = jnp.mean(v, axis=-1, keepdims=True)
    vc = v - mu
    var = jnp.mean(vc * vc, axis=-1, keepdims=True)
    vn = (vc * lax.rsqrt(var + LN_EPS) * lng_ref[...] + lnb_ref[...]).astype(BF16)
    for grp in range(GMLP_GROUPS):
        cs = slice(grp * c, (grp + 1) * c)
        w = jnp.where(causal, ws_ref[grp], 0.0).astype(BF16)
        bias = bs_ref[:, grp:grp + 1]
        for ch in range(rows // c):
            rs = slice(ch * c, (ch + 1) * c)
            mixed = _dot(w, vn[rs, cs]) + bias
            u = _gelu(u_ref[rs, cs].astype(F32))
            o_ref[rs, cs] = (u * mixed * _silu(g_ref[rs, cs].astype(F32))).astype(o_ref.dtype)


def _spatial_gating(proj, ln_g, ln_b, w_s, b_s_t, *, rows=256):
    m = proj.shape[0]
    d = D_MODEL
    return pl.pallas_call(
        functools.partial(_sgu_kernel, rows=rows),
        grid=(m // rows,),
        in_specs=[
            pl.BlockSpec((rows, d), lambda i: (i, 0)),
            pl.BlockSpec((rows, d), lambda i: (i, 1)),
            pl.BlockSpec((rows, d), lambda i: (i, 2)),
            pl.BlockSpec((1, d), lambda i: (0, 0)),
            pl.BlockSpec((1, d), lambda i: (0, 0)),
            pl.BlockSpec((GMLP_GROUPS, GMLP_CHUNK, GMLP_CHUNK), lambda i: (0, 0, 0)),
            pl.BlockSpec((GMLP_CHUNK, GMLP_GROUPS), lambda i: (0, 0)),
        ],
        out_specs=pl.BlockSpec((rows, d), lambda i: (i, 0)),
        out_shape=jax.ShapeDtypeStruct((m, d), BF16),
        compiler_params=pltpu.CompilerParams(
            dimension_semantics=("parallel",), vmem_limit_bytes=VMEM_LIMIT),
        name="spatial_gating",
    )(proj, proj, proj, ln_g, ln_b, w_s, b_s_t)


def _tail_kernel(a0_ref, a1_ref, h_ref, p_ref, w0_ref, w1_ref, gp_ref, wp_ref, wg_ref, o_ref):
    y = _dot(a0_ref[...], w0_ref[...]) + _dot(a1_ref[...], w1_ref[...])
    ms = jnp.mean(y * y, axis=-1, keepdims=True)
    h1 = h_ref[...] + y * lax.rsqrt(ms + RMS_EPS) * gp_ref[...]
    gate = _dot(h1.astype(BF16), wg_ref[...])
    pp = _dot(p_ref[...].astype(BF16), wp_ref[...])
    o_ref[...] = h1 + pp * jax.nn.sigmoid(gate)


def _tail(act0, blk0, act1, blk1, h, p, w_out, g_post, w_proj, w_gate, *, tm=512):
    m, d = h.shape
    half = w_out.shape[0] // 2
    return pl.pallas_call(
        _tail_kernel,
        grid=(m // tm,),
        in_specs=[
            pl.BlockSpec((tm, half), lambda i: (i, blk0)),
            pl.BlockSpec((tm, half), lambda i: (i, blk1)),
            pl.BlockSpec((tm, d), lambda i: (i, 0)),
            pl.BlockSpec((tm, PLE_DIM), lambda i: (i, 0)),
            _resident((half, d), lambda i: (0, 0)),
            _resident((half, d), lambda i: (1, 0)),
            _resident((1, d), lambda i: (0, 0)),
            _resident((PLE_DIM, d), lambda i: (0, 0)),
            _resident((d, d), lambda i: (0, 0)),
        ],
        out_specs=pl.BlockSpec((tm, d), lambda i: (i, 0)),
        out_shape=jax.ShapeDtypeStruct((m, d), F32),
        compiler_params=pltpu.CompilerParams(
            dimension_semantics=("parallel",), vmem_limit_bytes=VMEM_LIMIT),
        name="tail",
    )(act0, act1, h, p, w_out, w_out, g_post, w_proj, w_gate)


def _per_block_rows(vecs):
    nblk = HEAD_W // LANES
    rows = [v.reshape(nblk, 1, LANES) for v in vecs]
    rows += [jnp.zeros((nblk, 1, LANES), F32)] * (8 - len(rows))
    return jnp.concatenate(rows, axis=1)


def kernel(x, p, norm_pre, norm_post, ab_w_in, fox_f_bias, rwkv_mu, rwkv_w0, rwkv_w2, rwkv_a0, rwkv_a2, rwkv_k_k, rwkv_k_a, rwkv_r_k, rwkv_ln_g, rwkv_ln_b, ab_w_out, c_w_in, c_ln_g, c_ln_b, c_w_s, c_b_s, c_w_out, ple_w_proj, ple_w_gate):
    batch, seq, d = x.shape
    m = batch * seq
    w = HEAD_W
    h = x.reshape(m, d)

    w_in = ab_w_in[0]
    o_f = 3 * w
    o_ga = o_f + HEADS
    o_sh = o_ga + w
    o_lo = o_sh + 3 * w
    o_gb = o_lo + 2 * LORA
    w_main = jnp.concatenate(
        [w_in[:, :o_f], w_in[:, o_ga:o_sh], w_in[:, o_sh:o_lo], w_in[:, o_gb:]], axis=1).astype(BF16)
    w_small = jnp.concatenate(
        [w_in[:, o_lo:o_gb], w_in[:, o_f:o_ga], jnp.zeros((d, LANES - HEADS), F32)], axis=1).astype(BF16)
    main, small = _norm_matmul(h, norm_pre[0:1], w_main, w_small)
    main3 = main.reshape(batch, seq, 8 * w)
    small3 = small.reshape(batch, seq, 2 * LANES)

    f_bias = jnp.concatenate([fox_f_bias[0], jnp.zeros((LANES - HEADS,), F32)]).reshape(1, LANES)
    fox_bias = _fox_bias(small3, f_bias, batch, seq)
    act_a = _fox_attention(main3, fox_bias, batch, seq)

    mu = rwkv_mu[0]
    mu_rows = _per_block_rows([mu[:w], mu[w:2 * w], mu[2 * w:3 * w]])
    mu_rows = mu_rows.at[:, 3, :].set(jnp.broadcast_to(mu[3 * w:], (w // LANES, LANES)))
    par_rows = _per_block_rows([rwkv_w0[0], rwkv_a0[0], rwkv_k_k[0], rwkv_k_a[0],
                                rwkv_r_k[0].reshape(w), rwkv_ln_g[0], rwkv_ln_b[0]])
    act_b = _rwkv(main3, small3, mu_rows, par_rows, rwkv_w2[0].astype(BF16),
                  rwkv_a2[0].astype(BF16), batch, seq)

    h = _tail(act_a.reshape(m, w), 0, act_b.reshape(m, w), 0, h, p[0].reshape(m, PLE_DIM),
              ab_w_out[0].astype(BF16), norm_post[0:1], ple_w_proj[0].astype(BF16),
              ple_w_gate[0].astype(BF16))

    proj = _norm_matmul(h, norm_pre[1:2], c_w_in[0].astype(BF16))[0]
    act_c = _spatial_gating(proj, c_ln_g[0:1], c_ln_b[0:1], c_w_s[0], c_b_s[0].T)
    h = _tail(act_c, 0, act_c, 1, h, p[1].reshape(m, PLE_DIM),
              c_w_out[0].astype(BF16), norm_post[1:2], ple_w_proj[1].astype(BF16),
              ple_w_gate[1].astype(BF16))
    return h.reshape(batch, seq, d)
```

```python
import functools

import jax
import jax.numpy as jnp
from jax import lax
from jax.experimental import pallas as pl
from jax.experimental.pallas import tpu as pltpu

F32 = jnp.float32
BF16 = jnp.bfloat16

D_MODEL = 2048
PLE_DIM = 256
HEADS = 16
HEAD_DIM = 64
HEAD_W = HEADS * HEAD_DIM
LORA = 64
GMLP_GROUPS = 16
GMLP_CHUNK = 128
RMS_EPS = 1e-6
LN_EPS = 1e-5
RWKV_GN_EPS = 64e-5
LANES = 128
HEADS_PER_BLOCK = LANES // HEAD_DIM
RWKV_CHUNK = 64
VMEM_LIMIT = 56 * 1024 * 1024

NT_DIMS = (((1,), (1,)), ((), ()))
TN_DIMS = (((0,), (0,)), ((), ()))


def _dot(a, b):
    return jnp.dot(a, b, preferred_element_type=F32)


def _split(x, terms):
    parts = []
    for _ in range(terms - 1):
        part = x.astype(BF16)
        parts.append(part)
        x = x - part.astype(F32)
    parts.append(x.astype(BF16))
    return parts


def _dot_exact_lhs(a_bf16, x, terms=3):
    return sum(_dot(a_bf16, part) for part in _split(x, terms))


def _resident(block_shape, index_map):
    return pl.BlockSpec(block_shape, index_map, pipeline_mode=pl.Buffered(1))


def _silu(x):
    return x * jax.nn.sigmoid(x)


def _gelu(x):
    return 0.5 * x * (1.0 + lax.erf(x * (0.5 ** 0.5)))


def _norm_mm_kernel(x_ref, g_ref, w_ref, *rest, has_small):
    if has_small:
        ws_ref, o_ref, os_ref, xn_ref = rest
    else:
        o_ref, xn_ref = rest

    @pl.when(pl.program_id(1) == 0)
    def _():
        x = x_ref[...]
        ms = jnp.mean(x * x, axis=-1, keepdims=True)
        xn = (x * lax.rsqrt(ms + RMS_EPS) * g_ref[...]).astype(BF16)
        xn_ref[...] = xn
        if has_small:
            os_ref[...] = _dot(xn, ws_ref[...])

    o_ref[...] = _dot(xn_ref[...], w_ref[...]).astype(o_ref.dtype)


def _norm_matmul(x, g, w, w_small=None, *, tm=1024, tn=2048):
    m, d = x.shape
    n = w.shape[1]
    has_small = w_small is not None
    in_specs = [
        pl.BlockSpec((tm, d), lambda i, j: (i, 0)),
        _resident((1, d), lambda i, j: (0, 0)),
        pl.BlockSpec((d, tn), lambda i, j: (0, j)),
    ]
    out_specs = [pl.BlockSpec((tm, tn), lambda i, j: (i, j))]
    out_shape = [jax.ShapeDtypeStruct((m, n), BF16)]
    args = [x, g, w]
    if has_small:
        ns = w_small.shape[1]
        in_specs.append(_resident((d, ns), lambda i, j: (0, 0)))
        out_specs.append(pl.BlockSpec((tm, ns), lambda i, j: (i, 0)))
        out_shape.append(jax.ShapeDtypeStruct((m, ns), F32))
        args.append(w_small)
    return pl.pallas_call(
        functools.partial(_norm_mm_kernel, has_small=has_small),
        grid=(m // tm, n // tn),
        in_specs=in_specs,
        out_specs=out_specs,
        out_shape=out_shape,
        scratch_shapes=[pltpu.VMEM((tm, d), BF16)],
        compiler_params=pltpu.CompilerParams(
            dimension_semantics=("parallel", "arbitrary"), vmem_limit_bytes=VMEM_LIMIT),
        name="norm_matmul_small" if has_small else "norm_matmul",
    )(*args)


CUMSUM_BLOCK = 256
BIAS_TERMS = 3


def _bias_lane(head):
    return jnp.where(head % HEADS_PER_BLOCK == 0, HEAD_DIM, 0)


def _fox_bias_kernel(f_ref, b_ref, o_ref):
    t = f_ref.shape[1]
    row = lax.broadcasted_iota(jnp.int32, (CUMSUM_BLOCK, CUMSUM_BLOCK), 0)
    col = lax.broadcasted_iota(jnp.int32, (CUMSUM_BLOCK, CUMSUM_BLOCK), 1)
    tri = (row >= col).astype(BF16)
    head = lax.broadcasted_iota(jnp.int32, (LANES, HEAD_W), 0)
    lane = lax.broadcasted_iota(jnp.int32, (LANES, HEAD_W), 1)
    at_base = ((head < HEADS) & (lane // LANES == head // HEADS_PER_BLOCK))
    place = [(at_base & (lane % LANES == _bias_lane(head) + j)).astype(BF16)
             for j in range(BIAS_TERMS)]
    carry = jnp.zeros((1, LANES), F32)
    for blk in range(t // CUMSUM_BLOCK):
        sl = slice(blk * CUMSUM_BLOCK, (blk + 1) * CUMSUM_BLOCK)
        log_f = jax.nn.log_sigmoid(f_ref[0, sl, :] + b_ref[...])
        c = _dot_exact_lhs(tri, log_f) + carry
        carry = c[CUMSUM_BLOCK - 1:CUMSUM_BLOCK, :]
        terms = _split(c * (-(HEAD_DIM ** 0.5)), BIAS_TERMS)
        o_ref[0, sl, :] = sum(_dot(x, e) for x, e in zip(terms, place)).astype(o_ref.dtype)


def _fox_bias(small, f_bias_padded, batch, seq):
    return pl.pallas_call(
        _fox_bias_kernel,
        grid=(batch,),
        in_specs=[
            pl.BlockSpec((1, seq, LANES), lambda b: (b, 0, 1)),
            pl.BlockSpec((1, LANES), lambda b: (0, 0)),
        ],
        out_specs=pl.BlockSpec((1, seq, HEAD_W), lambda b: (b, 0, 0)),
        out_shape=jax.ShapeDtypeStruct((batch, seq, HEAD_W), BF16),
        compiler_params=pltpu.CompilerParams(
            dimension_semantics=("parallel",), vmem_limit_bytes=VMEM_LIMIT),
        name="fox_bias",
    )(small, f_bias_padded)


MASK_VALUE = -1e30
LOG2E = 1.4426950408889634


VT_ROWS = HEAD_DIM + 16


def _fox_attn_kernel(q_ref, k_ref, v_ref, bias_ref, g_ref, o_ref,
                     kaug_scr, vt_scr, *, blk):
    qi = pl.program_id(2)
    seq = k_ref.shape[1]

    @pl.when(qi == 0)
    def _():
        lane = lax.broadcasted_iota(jnp.int32, (seq, LANES), 1)
        k = k_ref[0]
        bias = bias_ref[0]
        kaug_scr[0] = jnp.where(lane < HEAD_DIM, k, bias)
        kaug_scr[1] = jnp.where(lane >= HEAD_DIM, k, bias)
        ones_row = (lax.broadcasted_iota(jnp.int32, (VT_ROWS - HEAD_DIM, blk), 0) == 0).astype(BF16)
        for j in range(seq // blk):
            v_t = v_ref[0, j * blk:(j + 1) * blk, :].astype(F32).T.astype(BF16)
            for h in range(HEADS_PER_BLOCK):
                vt_scr[j, h] = jnp.concatenate(
                    [v_t[h * HEAD_DIM:(h + 1) * HEAD_DIM], ones_row], axis=0)

    lane = lax.broadcasted_iota(jnp.int32, (blk, LANES), 1)
    q = q_ref[0]
    ones0 = ((lane >= HEAD_DIM) & (lane < HEAD_DIM + BIAS_TERMS)).astype(BF16)
    ones1 = (lane < BIAS_TERMS).astype(BF16)
    q_aug = jnp.stack([jnp.where(lane < HEAD_DIM, q, ones0), jnp.where(lane >= HEAD_DIM, q, ones1)])
    kappa = (HEAD_DIM ** -0.5) * LOG2E

    def scores(j):
        return _bdot_nt(kaug_scr[:, j * blk:(j + 1) * blk, :], q_aug)

    def softmax_step(s_t, m_prev, masked):
        if masked:
            key = lax.broadcasted_iota(jnp.int32, (blk, blk), 0)
            qry = lax.broadcasted_iota(jnp.int32, (blk, blk), 1)
            s_t = jnp.where(key <= qry, s_t, MASK_VALUE)
        m_new = jnp.maximum(m_prev, jnp.max(s_t, axis=1, keepdims=True))
        alpha = jnp.exp2((m_prev - m_new) * kappa)
        return jnp.exp2((s_t - m_new) * kappa).astype(BF16), alpha, m_new

    def attend(last):
        m = jnp.full((HEADS_PER_BLOCK, 1, blk), MASK_VALUE, F32)
        acc = jnp.zeros((HEADS_PER_BLOCK, VT_ROWS, blk), F32)
        s_next = scores(0)
        pending = None
        for j in range(last + 1):
            s_cur = s_next
            if j < last:
                s_next = scores(j + 1)
            p_t, alpha, m = softmax_step(s_cur, m, masked=(j == last))
            if pending is not None:
                acc = pending[2] * acc + _bdot(vt_scr[pending[0]], pending[1])
            pending = (j, p_t, alpha)
        acc = pending[2] * acc + _bdot(vt_scr[pending[0]], pending[1])
        o_t = acc[:, :HEAD_DIM, :] * (1.0 / acc[:, HEAD_DIM:HEAD_DIM + 1, :])
        o = o_t.reshape(LANES, blk).T
        o_ref[0] = (o * _silu(g_ref[0].astype(F32))).astype(o_ref.dtype)

    for last in range(seq // blk):
        pl.when(qi == last)(functools.partial(attend, last))


def _fox_attention(main, bias, batch, seq, *, blk=512):
    nblk = HEAD_W // LANES
    nq = seq // blk
    return pl.pallas_call(
        functools.partial(_fox_attn_kernel, blk=blk),
        grid=(batch, nblk, nq),
        in_specs=[
            pl.BlockSpec((1, blk, LANES), lambda b, hp, qi: (b, qi, hp)),
            pl.BlockSpec((1, seq, LANES), lambda b, hp, qi: (b, 0, nblk + hp)),
            pl.BlockSpec((1, seq, LANES), lambda b, hp, qi: (b, 0, 2 * nblk + hp)),
            pl.BlockSpec((1, seq, LANES), lambda b, hp, qi: (b, 0, hp)),
            pl.BlockSpec((1, blk, LANES), lambda b, hp, qi: (b, qi, 3 * nblk + hp)),
        ],
        out_specs=pl.BlockSpec((1, blk, LANES), lambda b, hp, qi: (b, qi, hp)),
        out_shape=jax.ShapeDtypeStruct((batch, seq, HEAD_W), BF16),
        scratch_shapes=[
            pltpu.VMEM((HEADS_PER_BLOCK, seq, LANES), BF16),
            pltpu.VMEM((seq // blk, HEADS_PER_BLOCK, VT_ROWS, blk), BF16),
        ],
        compiler_params=pltpu.CompilerParams(
            dimension_semantics=("parallel", "parallel", "arbitrary"),
            vmem_limit_bytes=VMEM_LIMIT),
        name="fox_attention",
    )(main, main, main, bias, main)


def _head_sum(x, terms):
    row = lax.broadcasted_iota(jnp.int32, (LANES, LANES), 0)
    col = lax.broadcasted_iota(jnp.int32, (LANES, LANES), 1)
    same_head = (row // HEAD_DIM == col // HEAD_DIM).astype(BF16)
    return sum(_dot(part, same_head) for part in _split(x, terms))


def _stack_heads(x):
    lane = lax.broadcasted_iota(jnp.int32, x.shape, x.ndim - 1)
    zero = jnp.zeros_like(x)
    return jnp.concatenate(
        [jnp.where(lane < HEAD_DIM, x, zero), jnp.where(lane >= HEAD_DIM, x, zero)], axis=x.ndim - 2)


def _bdot(a, b):
    return lax.dot_general(a, b, (((2,), (1,)), ((0,), (0,))), preferred_element_type=F32)


def _bdot_nt(a, b):
    return lax.dot_general(a, b, (((2,), (2,)), ((0,), (0,))), preferred_element_type=F32)


def _bdot_tn(a, b):
    return lax.dot_general(a, b, (((1,), (1,)), ((0,), (0,))), preferred_element_type=F32)


def _unit_lower_inverse(l_strict, order):
    c = l_strict.shape[1]
    eye = (lax.broadcasted_iota(jnp.int32, (c, LANES), 0)
           == lax.broadcasted_iota(jnp.int32, (c, LANES), 1) % HEAD_DIM).astype(F32)
    inv = eye + l_strict
    power = l_strict.astype(BF16)
    power = _bdot(power, _stack_heads(power)).astype(BF16)
    last = order.bit_length() - 2
    for k in range(1, last):
        both = _bdot(jnp.concatenate([inv.astype(BF16), power], axis=1), _stack_heads(power))
        inv = inv + both[:, :c]
        power = both[:, c:].astype(BF16)
    return inv + _bdot(inv.astype(BF16), _stack_heads(power))


RWKV_BLOCKS = 8


def _rwkv_kernel(r_ref, k_ref, v_ref, g_ref, wa_ref, mu_ref, par_ref, w2_ref, a2_ref,
                 o_ref, state_scr, prev_scr, *, tc):
    ti = pl.program_id(2)
    nb = RWKV_BLOCKS
    c = RWKV_CHUNK
    nc = tc // c

    @pl.when(ti == 0)
    def _():
        state_scr[...] = jnp.zeros(state_scr.shape, F32)
        prev_scr[...] = jnp.zeros(prev_scr.shape, F32)

    def blocks(ref, dtype=None):
        x = jnp.stack([ref[0, :, i * LANES:(i + 1) * LANES] for i in range(nb)])
        return x if dtype is None else x.astype(dtype)

    def shift_lerp(x, prev_row, mu):
        rolled = pltpu.roll(x, 1, x.ndim - 2)
        first = lax.broadcasted_iota(jnp.int32, x.shape, x.ndim - 2) == 0
        return x + (jnp.where(first, prev_row, rolled) - x) * mu

    def head_sum(x, terms):
        return _head_sum(x.reshape(nb * tc, LANES), terms).reshape(nb, tc, LANES)

    w0, a0, k_k, k_a, r_k, ln_g, ln_b = (par_ref[:, i:i + 1, :] for i in range(7))
    r_in, k_in, v_in = blocks(r_ref, F32), blocks(k_ref, F32), blocks(v_ref, F32)
    wa_in = wa_ref[0]
    r = shift_lerp(r_in, prev_scr[0], mu_ref[:, 0:1, :])
    k = shift_lerp(k_in, prev_scr[1], mu_ref[:, 1:2, :])
    v = shift_lerp(v_in, prev_scr[2], mu_ref[:, 2:3, :])
    wa = shift_lerp(wa_in, prev_scr[3, 0], mu_ref[0, 3:4, :])
    prev_scr[0] = r_in[:, tc - 1:tc, :]
    prev_scr[1] = k_in[:, tc - 1:tc, :]
    prev_scr[2] = v_in[:, tc - 1:tc, :]
    prev_scr[3, 0] = wa_in[tc - 1:tc, :]

    def per_block(x):
        return jnp.stack([x[:, i * LANES:(i + 1) * LANES] for i in range(nb)])

    w_lo = jnp.tanh(wa[:, :LORA]).astype(BF16)
    a_lo = wa[:, LORA:].astype(BF16)
    w_raw = -jax.nn.softplus(-(w0 + per_block(_dot(w_lo, w2_ref[...])))) - 0.5
    logw = -jnp.exp(w_raw)
    a = jax.nn.sigmoid(a0 + per_block(_dot(a_lo, a2_ref[...])))
    kk = k * k_k
    kk = kk * lax.rsqrt(jnp.maximum(head_sum(kk * kk, 1), 1e-24))
    k = k * (1.0 + (a - 1.0) * k_a)
    bonus = head_sum(r * k * r_k, 1) * v

    row = lax.broadcasted_iota(jnp.int32, (LANES, LANES), 0)
    col = lax.broadcasted_iota(jnp.int32, (LANES, LANES), 1)
    tri_blk = ((row >= col) & (row // c == col // c)).astype(BF16)
    cum = jnp.stack([
        jnp.concatenate([_dot_exact_lhs(tri_blk, logw[b, i:i + LANES], 2)
                         for i in range(0, tc, LANES)], axis=0)
        for b in range(nb)])

    trow = lax.broadcasted_iota(jnp.int32, (c, LANES), 0)
    tcol = lax.broadcasted_iota(jnp.int32, (c, LANES), 1) % HEAD_DIM
    lower_incl = trow >= tcol
    lower_strict = trow > tcol

    def chunked(x):
        return x.reshape(nb * nc, c, LANES)

    cum3 = chunked(cum)
    cum_end = cum3[:, c - 1:c, :]
    e_neg = jnp.exp(-cum3)
    e_end = jnp.exp(cum_end - cum3)
    akk = chunked(a * kk)
    k3 = chunked(k)
    r_t = chunked(r) * jnp.exp(cum3)
    p_t = (-chunked(kk) * jnp.exp(cum3 - chunked(logw))).astype(BF16)
    p_s = _stack_heads(p_t)
    z_s = _stack_heads((akk * e_neg).astype(BF16))
    k_s = _stack_heads((k3 * e_neg).astype(BF16))
    zbar_s = _stack_heads((akk * e_end).astype(BF16))
    kbar_s = _stack_heads((k3 * e_end).astype(BF16))
    v_s = _stack_heads(chunked(v).astype(BF16))
    pr = jnp.concatenate([p_t, r_t.astype(BF16)], axis=1)
    zk = jnp.concatenate([z_s, k_s], axis=1)
    amat = _bdot_nt(pr, zk)
    l_pz = jnp.where(lower_strict, amat[:, :c, :LANES], 0.0)
    l_pk = jnp.where(lower_strict, amat[:, :c, LANES:], 0.0).astype(BF16)
    l_rz = jnp.where(lower_incl, amat[:, c:, :LANES], 0.0).astype(BF16)
    l_rk = jnp.where(lower_incl, amat[:, c:, LANES:], 0.0).astype(BF16)
    inv = _unit_lower_inverse(l_pz, c).astype(BF16)
    lv = _bdot(l_pk, v_s).astype(BF16)
    tp_ui = _bdot(inv, jnp.concatenate([p_s, _stack_heads(lv)], axis=2)).astype(BF16)
    tp_ui_s = jnp.concatenate(
        [_stack_heads(tp_ui[:, :, :LANES]), _stack_heads(tp_ui[:, :, LANES:])], axis=2)
    rz = _bdot(l_rz, tp_ui_s)
    qeff = (r_t + rz[:, :, :LANES]).astype(BF16)
    yi = rz[:, :, LANES:] + _bdot(l_rk, v_s)
    mz = _bdot_tn(tp_ui_s, zbar_s)
    m_low = mz[:, :LANES].astype(BF16)
    b_mat = mz[:, LANES:] + _bdot_tn(v_s, kbar_s)
    g_end = jnp.exp(cum_end)

    def of_chunk(x, ci):
        return x.reshape((nb, nc) + x.shape[1:])[:, ci]

    s = state_scr[...]
    ys = []
    for ci in range(nc):
        s_b = s.astype(BF16)
        ys.append(_bdot_nt(of_chunk(qeff, ci), s_b) + of_chunk(yi, ci))
        s = s * of_chunk(g_end, ci) + _bdot(s_b, of_chunk(m_low, ci)) + of_chunk(b_mat, ci)
    state_scr[...] = s

    y = jnp.concatenate(ys, axis=1)
    mean = head_sum(y, 2) * (1.0 / HEAD_DIM)
    yc = y - mean
    var = head_sum(yc * yc, 1) * (1.0 / HEAD_DIM)
    y = yc * lax.rsqrt(var + RWKV_GN_EPS) * ln_g + ln_b + bonus
    y = y * _silu(blocks(g_ref, F32))
    for b in range(nb):
        o_ref[0, :, b * LANES:(b + 1) * LANES] = y[b].astype(o_ref.dtype)


def _rwkv(main, small, mu_rows, par_rows, w2, a2, batch, seq, *, tc=128):
    nb = RWKV_BLOCKS
    wide = nb * LANES
    ngrp = HEAD_W // wide
    return pl.pallas_call(
        functools.partial(_rwkv_kernel, tc=tc),
        grid=(batch, ngrp, seq // tc),
        in_specs=[
            pl.BlockSpec((1, tc, wide), lambda b, hg, ti: (b, ti, 4 * ngrp + hg)),
            pl.BlockSpec((1, tc, wide), lambda b, hg, ti: (b, ti, 5 * ngrp + hg)),
            pl.BlockSpec((1, tc, wide), lambda b, hg, ti: (b, ti, 6 * ngrp + hg)),
            pl.BlockSpec((1, tc, wide), lambda b, hg, ti: (b, ti, 7 * ngrp + hg)),
            pl.BlockSpec((1, tc, LANES), lambda b, hg, ti: (b, ti, 0)),
            pl.BlockSpec((nb, 8, LANES), lambda b, hg, ti: (hg, 0, 0)),
            pl.BlockSpec((nb, 8, LANES), lambda b, hg, ti: (hg, 0, 0)),
            pl.BlockSpec((LORA, wide), lambda b, hg, ti: (0, hg)),
            pl.BlockSpec((LORA, wide), lambda b, hg, ti: (0, hg)),
        ],
        out_specs=pl.BlockSpec((1, tc, wide), lambda b, hg, ti: (b, ti, hg)),
        out_shape=jax.ShapeDtypeStruct((batch, seq, HEAD_W), BF16),
        scratch_shapes=[
            pltpu.VMEM((nb, LANES, LANES), F32),
            pltpu.VMEM((4, nb, 1, LANES), F32),
        ],
        compiler_params=pltpu.CompilerParams(
            dimension_semantics=("parallel", "parallel", "arbitrary"),
            vmem_limit_bytes=VMEM_LIMIT),
        name="rwkv7",
    )(main, main, main, main, small, mu_rows, par_rows, w2, a2)


def _sgu_kernel(u_ref, v_ref, g_ref, lng_ref, lnb_ref, ws_ref, bs_ref, o_ref, *, rows):
    c = GMLP_CHUNK
    causal = (lax.broadcasted_iota(jnp.int32, (c, c), 0)
              >= lax.broadcasted_iota(jnp.int32, (c, c), 1))
    v = _gelu(v_ref[...].astype(F32))
    mu = jnp.mean(v, axis=-1, keepdims=True)
    vc = v - mu
    var = jnp.mean(vc * vc, axis=-1, keepdims=True)
    vn = (vc * lax.rsqrt(var + LN_EPS) * lng_ref[...] + lnb_ref[...]).astype(BF16)
    for grp in range(GMLP_GROUPS):
        cs = slice(grp * c, (grp + 1) * c)
        w = jnp.where(causal, ws_ref[grp], 0.0).astype(BF16)
        bias = bs_ref[:, grp:grp + 1]
        for ch in range(rows // c):
            rs = slice(ch * c, (ch + 1) * c)
            mixed = _dot(w, vn[rs, cs]) + bias
            u = _gelu(u_ref[rs, cs].astype(F32))
            o_ref[rs, cs] = (u * mixed * _silu(g_ref[rs, cs].astype(F32))).astype(o_ref.dtype)


def _spatial_gating(proj, ln_g, ln_b, w_s, b_s_t, *, rows=256):
    m = proj.shape[0]
    d = D_MODEL
    return pl.pallas_call(
        functools.partial(_sgu_kernel, rows=rows),
        grid=(m // rows,),
        in_specs=[
            pl.BlockSpec((rows, d), lambda i: (i, 0)),
            pl.BlockSpec((rows, d), lambda i: (i, 1)),
            pl.BlockSpec((rows, d), lambda i: (i, 2)),
            pl.BlockSpec((1, d), lambda i: (0, 0)),
            pl.BlockSpec((1, d), lambda i: (0, 0)),
            pl.BlockSpec((GMLP_GROUPS, GMLP_CHUNK, GMLP_CHUNK), lambda i: (0, 0, 0)),
            pl.BlockSpec((GMLP_CHUNK, GMLP_GROUPS), lambda i: (0, 0)),
        ],
        out_specs=pl.BlockSpec((rows, d), lambda i: (i, 0)),
        out_shape=jax.ShapeDtypeStruct((m, d), BF16),
        compiler_params=pltpu.CompilerParams(
            dimension_semantics=("parallel",), vmem_limit_bytes=VMEM_LIMIT),
        name="spatial_gating",
    )(proj, proj, proj, ln_g, ln_b, w_s, b_s_t)


def _tail_kernel(a0_ref, a1_ref, h_ref, p_ref, w0_ref, w1_ref, gp_ref, wp_ref, wg_ref, o_ref):
    y = _dot(a0_ref[...], w0_ref[...]) + _dot(a1_ref[...], w1_ref[...])
    ms = jnp.mean(y * y, axis=-1, keepdims=True)
    h1 = h_ref[...] + y * lax.rsqrt(ms + RMS_EPS) * gp_ref[...]
    gate = _dot(h1.astype(BF16), wg_ref[...])
    pp = _dot(p_ref[...].astype(BF16), wp_ref[...])
    o_ref[...] = h1 + pp * jax.nn.sigmoid(gate)


def _tail(act0, blk0, act1, blk1, h, p, w_out, g_post, w_proj, w_gate, *, tm=512):
    m, d = h.shape
    half = w_out.shape[0] // 2
    return pl.pallas_call(
        _tail_kernel,
        grid=(m // tm,),
        in_specs=[
            pl.BlockSpec((tm, half), lambda i: (i, blk0)),
            pl.BlockSpec((tm, half), lambda i: (i, blk1)),
            pl.BlockSpec((tm, d), lambda i: (i, 0)),
            pl.BlockSpec((tm, PLE_DIM), lambda i: (i, 0)),
            _resident((half, d), lambda i: (0, 0)),
            _resident((half, d), lambda i: (1, 0)),
            _resident((1, d), lambda i: (0, 0)),
            _resident((PLE_DIM, d), lambda i: (0, 0)),
            _resident((d, d), lambda i: (0, 0)),
        ],
        out_specs=pl.BlockSpec((tm, d), lambda i: (i, 0)),
        out_shape=jax.ShapeDtypeStruct((m, d), F32),
        compiler_params=pltpu.CompilerParams(
            dimension_semantics=("parallel",), vmem_limit_bytes=VMEM_LIMIT),
        name="tail",
    )(act0, act1, h, p, w_out, w_out, g_post, w_proj, w_gate)


def _per_block_rows(vecs):
    nblk = HEAD_W // LANES
    rows = [v.reshape(nblk, 1, LANES) for v in vecs]
    rows += [jnp.zeros((nblk, 1, LANES), F32)] * (8 - len(rows))
    return jnp.concatenate(rows, axis=1)


def kernel(x, p, norm_pre, norm_post, ab_w_in, fox_f_bias, rwkv_mu, rwkv_w0, rwkv_w2, rwkv_a0, rwkv_a2, rwkv_k_k, rwkv_k_a, rwkv_r_k, rwkv_ln_g, rwkv_ln_b, ab_w_out, c_w_in, c_ln_g, c_ln_b, c_w_s, c_b_s, c_w_out, ple_w_proj, ple_w_gate):
    batch, seq, d = x.shape
    m = batch * seq
    w = HEAD_W
    h = x.reshape(m, d)

    w_in = ab_w_in[0]
    o_f = 3 * w
    o_ga = o_f + HEADS
    o_sh = o_ga + w
    o_lo = o_sh + 3 * w
    o_gb = o_lo + 2 * LORA
    w_main = jnp.concatenate(
        [w_in[:, :o_f], w_in[:, o_ga:o_sh], w_in[:, o_sh:o_lo], w_in[:, o_gb:]], axis=1).astype(BF16)
    w_small = jnp.concatenate(
        [w_in[:, o_lo:o_gb], w_in[:, o_f:o_ga], jnp.zeros((d, LANES - HEADS), F32)], axis=1).astype(BF16)
    main, small = _norm_matmul(h, norm_pre[0:1], w_main, w_small)
    main3 = main.reshape(batch, seq, 8 * w)
    small3 = small.reshape(batch, seq, 2 * LANES)

    f_bias = jnp.concatenate([fox_f_bias[0], jnp.zeros((LANES - HEADS,), F32)]).reshape(1, LANES)
    fox_bias = _fox_bias(small3, f_bias, batch, seq)
    act_a = _fox_attention(main3, fox_bias, batch, seq)

    mu = rwkv_mu[0]
    mu_rows = _per_block_rows([mu[:w], mu[w:2 * w], mu[2 * w:3 * w]])
    mu_rows = mu_rows.at[:, 3, :].set(jnp.broadcast_to(mu[3 * w:], (w // LANES, LANES)))
    par_rows = _per_block_rows([rwkv_w0[0], rwkv_a0[0], rwkv_k_k[0], rwkv_k_a[0],
                                rwkv_r_k[0].reshape(w), rwkv_ln_g[0], rwkv_ln_b[0]])
    act_b = _rwkv(main3, small3, mu_rows, par_rows, rwkv_w2[0].astype(BF16),
                  rwkv_a2[0].astype(BF16), batch, seq)

    h = _tail(act_a.reshape(m, w), 0, act_b.reshape(m, w), 0, h, p[0].reshape(m, PLE_DIM),
              ab_w_out[0].astype(BF16), norm_post[0:1], ple_w_proj[0].astype(BF16),
              ple_w_gate[0].astype(BF16))

    proj = _norm_matmul(h, norm_pre[1:2], c_w_in[0].astype(BF16))[0]
    act_c = _spatial_gating(proj, c_ln_g[0:1], c_ln_b[0:1], c_w_s[0], c_b_s[0].T)
    h = _tail(act_c, 0, act_c, 1, h, p[1].reshape(m, PLE_DIM),
              c_w_out[0].astype(BF16), norm_post[1:2], ple_w_proj[1].astype(BF16),
              ple_w_gate[1].astype(BF16))
    return h.reshape(batch, seq, d)
```

```python
import functools

import jax
import jax.numpy as jnp
from jax import lax
from jax.experimental import pallas as pl
from jax.experimental.pallas import tpu as pltpu

F32 = jnp.float32
BF16 = jnp.bfloat16

D_MODEL = 2048
PLE_DIM = 256
HEADS = 16
HEAD_DIM = 64
HEAD_W = HEADS * HEAD_DIM
LORA = 64
GMLP_GROUPS = 16
GMLP_CHUNK = 128
RMS_EPS = 1e-6
LN_EPS = 1e-5
RWKV_GN_EPS = 64e-5
LANES = 128
HEADS_PER_BLOCK = LANES // HEAD_DIM
RWKV_CHUNK = 64
VMEM_LIMIT = 56 * 1024 * 1024

NT_DIMS = (((1,), (1,)), ((), ()))
TN_DIMS = (((0,), (0,)), ((), ()))


def _dot(a, b):
    return jnp.dot(a, b, preferred_element_type=F32)


def _split(x, terms):
    parts = []
    for _ in range(terms - 1):
        part = x.astype(BF16)
        parts.append(part)
        x = x - part.astype(F32)
    parts.append(x.astype(BF16))
    return parts


def _dot_exact_lhs(a_bf16, x, terms=3):
    return sum(_dot(a_bf16, part) for part in _split(x, terms))


def _resident(block_shape, index_map):
    return pl.BlockSpec(block_shape, index_map, pipeline_mode=pl.Buffered(1))


def _silu(x):
    return x * jax.nn.sigmoid(x)


def _gelu(x):
    return 0.5 * x * (1.0 + lax.erf(x * (0.5 ** 0.5)))


def _norm_mm_kernel(x_ref, g_ref, w_ref, *rest, has_small):
    if has_small:
        ws_ref, o_ref, os_ref, xn_ref = rest
    else:
        o_ref, xn_ref = rest

    @pl.when(pl.program_id(1) == 0)
    def _():
        x = x_ref[...]
        ms = jnp.mean(x * x, axis=-1, keepdims=True)
        xn = (x * lax.rsqrt(ms + RMS_EPS) * g_ref[...]).astype(BF16)
        xn_ref[...] = xn
        if has_small:
            os_ref[...] = _dot(xn, ws_ref[...])

    o_ref[...] = _dot(xn_ref[...], w_ref[...]).astype(o_ref.dtype)


def _norm_matmul(x, g, w, w_small=None, *, tm=1024, tn=2048):
    m, d = x.shape
    n = w.shape[1]
    has_small = w_small is not None
    in_specs = [
        pl.BlockSpec((tm, d), lambda i, j: (i, 0)),
        _resident((1, d), lambda i, j: (0, 0)),
        pl.BlockSpec((d, tn), lambda i, j: (0, j)),
    ]
    out_specs = [pl.BlockSpec((tm, tn), lambda i, j: (i, j))]
    out_shape = [jax.ShapeDtypeStruct((m, n), BF16)]
    args = [x, g, w]
    if has_small:
        ns = w_small.shape[1]
        in_specs.append(_resident((d, ns), lambda i, j: (0, 0)))
        out_specs.append(pl.BlockSpec((tm, ns), lambda i, j: (i, 0)))
        out_shape.append(jax.ShapeDtypeStruct((m, ns), F32))
        args.append(w_small)
    return pl.pallas_call(
        functools.partial(_norm_mm_kernel, has_small=has_small),
        grid=(m // tm, n // tn),
        in_specs=in_specs,
        out_specs=out_specs,
        out_shape=out_shape,
        scratch_shapes=[pltpu.VMEM((tm, d), BF16)],
        compiler_params=pltpu.CompilerParams(
            dimension_semantics=("parallel", "arbitrary"), vmem_limit_bytes=VMEM_LIMIT),
        name="norm_matmul_small" if has_small else "norm_matmul",
    )(*args)


CUMSUM_BLOCK = 256
BIAS_TERMS = 3


def _bias_lane(head):
    return jnp.where(head % HEADS_PER_BLOCK == 0, HEAD_DIM, 0)


def _fox_bias_kernel(f_ref, b_ref, o_ref):
    t = f_ref.shape[1]
    row = lax.broadcasted_iota(jnp.int32, (CUMSUM_BLOCK, CUMSUM_BLOCK), 0)
    col = lax.broadcasted_iota(jnp.int32, (CUMSUM_BLOCK, CUMSUM_BLOCK), 1)
    tri = (row >= col).astype(BF16)
    head = lax.broadcasted_iota(jnp.int32, (LANES, HEAD_W), 0)
    lane = lax.broadcasted_iota(jnp.int32, (LANES, HEAD_W), 1)
    at_base = ((head < HEADS) & (lane // LANES == head // HEADS_PER_BLOCK))
    place = [(at_base & (lane % LANES == _bias_lane(head) + j)).astype(BF16)
             for j in range(BIAS_TERMS)]
    carry = jnp.zeros((1, LANES), F32)
    for blk in range(t // CUMSUM_BLOCK):
        sl = slice(blk * CUMSUM_BLOCK, (blk + 1) * CUMSUM_BLOCK)
        log_f = jax.nn.log_sigmoid(f_ref[0, sl, :] + b_ref[...])
        c = _dot_exact_lhs(tri, log_f) + carry
        carry = c[CUMSUM_BLOCK - 1:CUMSUM_BLOCK, :]
        terms = _split(c * (-(HEAD_DIM ** 0.5)), BIAS_TERMS)
        o_ref[0, sl, :] = sum(_dot(x, e) for x, e in zip(terms, place)).astype(o_ref.dtype)


def _fox_bias(small, f_bias_padded, batch, seq):
    return pl.pallas_call(
        _fox_bias_kernel,
        grid=(batch,),
        in_specs=[
            pl.BlockSpec((1, seq, LANES), lambda b: (b, 0, 1)),
            pl.BlockSpec((1, LANES), lambda b: (0, 0)),
        ],
        out_specs=pl.BlockSpec((1, seq, HEAD_W), lambda b: (b, 0, 0)),
        out_shape=jax.ShapeDtypeStruct((batch, seq, HEAD_W), BF16),
        compiler_params=pltpu.CompilerParams(
            dimension_semantics=("parallel",), vmem_limit_bytes=VMEM_LIMIT),
        name="fox_bias",
    )(small, f_bias_padded)


MASK_VALUE = -1e30
LOG2E = 1.4426950408889634


VT_ROWS = HEAD_DIM + 16
FOX_BLOCKS = 2


def _fox_attn_kernel(q_ref, k_ref, v_ref, bias_ref, g_ref, o_ref,
                     kaug_scr, vt_scr, *, blk):
    qi = pl.program_id(2)
    seq = k_ref.shape[1]
    nh = FOX_BLOCKS * HEADS_PER_BLOCK

    @pl.when(qi == 0)
    def _():
        lane = lax.broadcasted_iota(jnp.int32, (seq, LANES), 1)
        ones_row = (lax.broadcasted_iota(jnp.int32, (VT_ROWS - HEAD_DIM, blk), 0) == 0).astype(BF16)
        for p in range(FOX_BLOCKS):
            ls = slice(p * LANES, (p + 1) * LANES)
            k = k_ref[0, :, ls]
            bias = bias_ref[0, :, ls]
            kaug_scr[HEADS_PER_BLOCK * p] = jnp.where(lane < HEAD_DIM, k, bias)
            kaug_scr[HEADS_PER_BLOCK * p + 1] = jnp.where(lane >= HEAD_DIM, k, bias)
            for j in range(seq // blk):
                v_t = v_ref[0, j * blk:(j + 1) * blk, ls].T
                for h in range(HEADS_PER_BLOCK):
                    vt_scr[j, HEADS_PER_BLOCK * p + h] = jnp.concatenate(
                        [v_t[h * HEAD_DIM:(h + 1) * HEAD_DIM], ones_row], axis=0)

    lane = lax.broadcasted_iota(jnp.int32, (blk, LANES), 1)
    ones0 = ((lane >= HEAD_DIM) & (lane < HEAD_DIM + BIAS_TERMS)).astype(BF16)
    ones1 = (lane < BIAS_TERMS).astype(BF16)
    q_aug = []
    for p in range(FOX_BLOCKS):
        q = q_ref[0, :, p * LANES:(p + 1) * LANES]
        q_aug += [jnp.where(lane < HEAD_DIM, q, ones0), jnp.where(lane >= HEAD_DIM, q, ones1)]
    q_aug = jnp.stack(q_aug)
    kappa = (HEAD_DIM ** -0.5) * LOG2E

    half = blk // 2

    def scores(step):
        (k0, k1), (q0, q1) = step
        return _bdot_nt(kaug_scr[:, k0:k1, :], q_aug[:, q0:q1, :])

    def softmax_step(s_t, m_prev, masked):
        if masked:
            key = lax.broadcasted_iota(jnp.int32, s_t.shape[1:], 0)
            qry = lax.broadcasted_iota(jnp.int32, s_t.shape[1:], 1)
            s_t = jnp.where(key <= qry, s_t, MASK_VALUE)
        m_new = jnp.maximum(m_prev, jnp.max(s_t, axis=1, keepdims=True))
        alpha = jnp.exp2((m_prev - m_new) * kappa)
        return jnp.exp2((s_t - m_new) * kappa).astype(BF16), alpha, m_new

    def attend(last):
        steps = [((j * blk, (j + 1) * blk), (0, blk)) for j in range(last)]
        steps += [((last * blk, last * blk + half), (0, blk)),
                  ((last * blk + half, (last + 1) * blk), (half, blk))]
        m = jnp.full((nh, 1, blk), MASK_VALUE, F32)
        acc = jnp.zeros((nh, VT_ROWS, blk), F32)

        def accumulate(acc, step, p_t, alpha):
            (k0, k1), (q0, q1) = step
            v_t = vt_scr[k0 // blk][:, :, k0 % blk:k0 % blk + (k1 - k0)]
            new = alpha * acc[:, :, q0:q1] + _bdot(v_t, p_t)
            return new if q0 == 0 else jnp.concatenate([acc[:, :, :q0], new], axis=2)

        s_next = scores(steps[0])
        pending = None
        for i, step in enumerate(steps):
            s_cur = s_next
            if i + 1 < len(steps):
                s_next = scores(steps[i + 1])
            q0, q1 = step[1]
            p_t, alpha, m_part = softmax_step(s_cur, m[:, :, q0:q1], masked=(i >= last))
            m = m_part if q0 == 0 else jnp.concatenate([m[:, :, :q0], m_part], axis=2)
            if pending is not None:
                acc = accumulate(acc, *pending)
            pending = (step, p_t, alpha)
        acc = accumulate(acc, *pending)
        o_t = acc[:, :HEAD_DIM, :] * (1.0 / acc[:, HEAD_DIM:HEAD_DIM + 1, :])
        o = o_t.reshape(nh * HEAD_DIM, blk).T
        o_ref[0] = (o * _silu(g_ref[0].astype(F32))).astype(o_ref.dtype)

    for last in range(seq // blk):
        pl.when(qi == last)(functools.partial(attend, last))


def _fox_attention(main, bias, batch, seq, *, blk=512):
    wide = FOX_BLOCKS * LANES
    nblk = HEAD_W // wide
    nq = seq // blk
    return pl.pallas_call(
        functools.partial(_fox_attn_kernel, blk=blk),
        grid=(batch, nblk, nq),
        in_specs=[
            pl.BlockSpec((1, blk, wide), lambda b, hp, qi: (b, qi, hp)),
            pl.BlockSpec((1, seq, wide), lambda b, hp, qi: (b, 0, nblk + hp)),
            pl.BlockSpec((1, seq, wide), lambda b, hp, qi: (b, 0, 2 * nblk + hp)),
            pl.BlockSpec((1, seq, wide), lambda b, hp, qi: (b, 0, hp)),
            pl.BlockSpec((1, blk, wide), lambda b, hp, qi: (b, qi, 3 * nblk + hp)),
        ],
        out_specs=pl.BlockSpec((1, blk, wide), lambda b, hp, qi: (b, qi, hp)),
        out_shape=jax.ShapeDtypeStruct((batch, seq, HEAD_W), BF16),
        scratch_shapes=[
            pltpu.VMEM((FOX_BLOCKS * HEADS_PER_BLOCK, seq, LANES), BF16),
            pltpu.VMEM((seq // blk, FOX_BLOCKS * HEADS_PER_BLOCK, VT_ROWS, blk), BF16),
        ],
        compiler_params=pltpu.CompilerParams(
            dimension_semantics=("parallel", "parallel", "arbitrary"),
            vmem_limit_bytes=VMEM_LIMIT),
        name="fox_attention",
    )(main, main, main, bias, main)


def _head_sum(x, terms):
    row = lax.broadcasted_iota(jnp.int32, (LANES, LANES), 0)
    col = lax.broadcasted_iota(jnp.int32, (LANES, LANES), 1)
    same_head = (row // HEAD_DIM == col // HEAD_DIM).astype(BF16)
    return sum(_dot(part, same_head) for part in _split(x, terms))


def _stack_heads(x):
    lane = lax.broadcasted_iota(jnp.int32, x.shape, x.ndim - 1)
    zero = jnp.zeros_like(x)
    return jnp.concatenate(
        [jnp.where(lane < HEAD_DIM, x, zero), jnp.where(lane >= HEAD_DIM, x, zero)], axis=x.ndim - 2)


def _bdot(a, b):
    return lax.dot_general(a, b, (((2,), (1,)), ((0,), (0,))), preferred_element_type=F32)


def _bdot_nt(a, b):
    return lax.dot_general(a, b, (((2,), (2,)), ((0,), (0,))), preferred_element_type=F32)


def _bdot_tn(a, b):
    return lax.dot_general(a, b, (((1,), (1,)), ((0,), (0,))), preferred_element_type=F32)


def _unit_lower_inverse(l_strict, order):
    c = l_strict.shape[1]
    eye = (lax.broadcasted_iota(jnp.int32, (c, LANES), 0)
           == lax.broadcasted_iota(jnp.int32, (c, LANES), 1) % HEAD_DIM).astype(F32)
    inv = eye + l_strict
    power = l_strict.astype(BF16)
    power = _bdot(power, _stack_heads(power)).astype(BF16)
    last = order.bit_length() - 2
    for k in range(1, last):
        both = _bdot(jnp.concatenate([inv.astype(BF16), power], axis=1), _stack_heads(power))
        inv = inv + both[:, :c]
        power = both[:, c:].astype(BF16)
    return inv + _bdot(inv.astype(BF16), _stack_heads(power))


RWKV_BLOCKS = 8


def _rwkv_kernel(r_ref, k_ref, v_ref, g_ref, wa_ref, mu_ref, par_ref, w2_ref, a2_ref,
                 o_ref, state_scr, prev_scr, *, tc):
    ti = pl.program_id(2)
    nb = RWKV_BLOCKS
    c = RWKV_CHUNK
    nc = tc // c

    @pl.when(ti == 0)
    def _():
        state_scr[...] = jnp.zeros(state_scr.shape, F32)
        prev_scr[...] = jnp.zeros(prev_scr.shape, F32)

    def blocks(ref, dtype=None):
        x = jnp.stack([ref[0, :, i * LANES:(i + 1) * LANES] for i in range(nb)])
        return x if dtype is None else x.astype(dtype)

    def shift_lerp(x, prev_row, mu):
        rolled = pltpu.roll(x, 1, x.ndim - 2)
        first = lax.broadcasted_iota(jnp.int32, x.shape, x.ndim - 2) == 0
        return x + (jnp.where(first, prev_row, rolled) - x) * mu

    def head_sum(x, terms):
        return _head_sum(x.reshape(nb * tc, LANES), terms).reshape(nb, tc, LANES)

    w0, a0, k_k, k_a, r_k, ln_g, ln_b = (par_ref[:, i:i + 1, :] for i in range(7))
    r_in, k_in, v_in = blocks(r_ref, F32), blocks(k_ref, F32), blocks(v_ref, F32)
    wa_in = wa_ref[0]
    r = shift_lerp(r_in, prev_scr[0], mu_ref[:, 0:1, :])
    k = shift_lerp(k_in, prev_scr[1], mu_ref[:, 1:2, :])
    v = shift_lerp(v_in, prev_scr[2], mu_ref[:, 2:3, :])
    wa = shift_lerp(wa_in, prev_scr[3, 0], mu_ref[0, 3:4, :])
    prev_scr[0] = r_in[:, tc - 1:tc, :]
    prev_scr[1] = k_in[:, tc - 1:tc, :]
    prev_scr[2] = v_in[:, tc - 1:tc, :]
    prev_scr[3, 0] = wa_in[tc - 1:tc, :]

    def per_block(x):
        return jnp.stack([x[:, i * LANES:(i + 1) * LANES] for i in range(nb)])

    w_lo = jnp.tanh(wa[:, :LORA]).astype(BF16)
    a_lo = wa[:, LORA:].astype(BF16)
    w_raw = -jax.nn.softplus(-(w0 + per_block(_dot(w_lo, w2_ref[...])))) - 0.5
    logw = -jnp.exp(w_raw)
    a = jax.nn.sigmoid(a0 + per_block(_dot(a_lo, a2_ref[...])))
    kk = k * k_k
    kk = kk * lax.rsqrt(jnp.maximum(head_sum(kk * kk, 1), 1e-24))
    k = k * (1.0 + (a - 1.0) * k_a)
    bonus = head_sum(r * k * r_k, 1) * v

    row = lax.broadcasted_iota(jnp.int32, (LANES, LANES), 0)
    col = lax.broadcasted_iota(jnp.int32, (LANES, LANES), 1)
    tri_blk = ((row >= col) & (row // c == col // c)).astype(BF16)
    cum = jnp.stack([
        jnp.concatenate([_dot_exact_lhs(tri_blk, logw[b, i:i + LANES], 2)
                         for i in range(0, tc, LANES)], axis=0)
        for b in range(nb)])

    trow = lax.broadcasted_iota(jnp.int32, (c, LANES), 0)
    tcol = lax.broadcasted_iota(jnp.int32, (c, LANES), 1) % HEAD_DIM
    lower_incl = trow >= tcol
    lower_strict = trow > tcol

    def chunked(x):
        return x.reshape(nb * nc, c, LANES)

    cum3 = chunked(cum)
    cum_end = cum3[:, c - 1:c, :]
    e_neg = jnp.exp(-cum3)
    e_end = jnp.exp(cum_end - cum3)
    akk = chunked(a * kk)
    k3 = chunked(k)
    r_t = chunked(r) * jnp.exp(cum3)
    p_t = (-chunked(kk) * jnp.exp(cum3 - chunked(logw))).astype(BF16)
    p_s = _stack_heads(p_t)
    z_s = _stack_heads((akk * e_neg).astype(BF16))
    k_s = _stack_heads((k3 * e_neg).astype(BF16))
    zbar_s = _stack_heads((akk * e_end).astype(BF16))
    kbar_s = _stack_heads((k3 * e_end).astype(BF16))
    v_s = _stack_heads(chunked(v).astype(BF16))
    pr = jnp.concatenate([p_t, r_t.astype(BF16)], axis=1)
    zk = jnp.concatenate([z_s, k_s], axis=1)
    amat = _bdot_nt(pr, zk)
    l_pz = jnp.where(lower_strict, amat[:, :c, :LANES], 0.0)
    l_pk = jnp.where(lower_strict, amat[:, :c, LANES:], 0.0).astype(BF16)
    l_rz = jnp.where(lower_incl, amat[:, c:, :LANES], 0.0).astype(BF16)
    l_rk = jnp.where(lower_incl, amat[:, c:, LANES:], 0.0).astype(BF16)
    inv = _unit_lower_inverse(l_pz, c).astype(BF16)
    lv = _bdot(l_pk, v_s).astype(BF16)
    tp_ui = _bdot(inv, jnp.concatenate([p_s, _stack_heads(lv)], axis=2)).astype(BF16)
    tp_ui_s = jnp.concatenate(
        [_stack_heads(tp_ui[:, :, :LANES]), _stack_heads(tp_ui[:, :, LANES:])], axis=2)
    rz = _bdot(l_rz, tp_ui_s)
    qeff = (r_t + rz[:, :, :LANES]).astype(BF16)
    yi = rz[:, :, LANES:] + _bdot(l_rk, v_s)
    mz = _bdot_tn(tp_ui_s, zbar_s)
    m_low = mz[:, :LANES].astype(BF16)
    b_mat = mz[:, LANES:] + _bdot_tn(v_s, kbar_s)
    g_end = jnp.exp(cum_end)

    def of_chunk(x, ci):
        return x.reshape((nb, nc) + x.shape[1:])[:, ci]

    s = state_scr[...]
    ys = []
    for ci in range(nc):
        s_b = s.astype(BF16)
        ys.append(_bdot_nt(of_chunk(qeff, ci), s_b) + of_chunk(yi, ci))
        s = s * of_chunk(g_end, ci) + _bdot(s_b, of_chunk(m_low, ci)) + of_chunk(b_mat, ci)
    state_scr[...] = s

    y = jnp.concatenate(ys, axis=1)
    mean = head_sum(y, 2) * (1.0 / HEAD_DIM)
    yc = y - mean
    var = head_sum(yc * yc, 1) * (1.0 / HEAD_DIM)
    y = yc * lax.rsqrt(var + RWKV_GN_EPS) * ln_g + ln_b + bonus
    y = y * _silu(blocks(g_ref, F32))
    for b in range(nb):
        o_ref[0, :, b * LANES:(b + 1) * LANES] = y[b].astype(o_ref.dtype)


def _rwkv(main, small, mu_rows, par_rows, w2, a2, batch, seq, *, tc=128):
    nb = RWKV_BLOCKS
    wide = nb * LANES
    ngrp = HEAD_W // wide
    return pl.pallas_call(
        functools.partial(_rwkv_kernel, tc=tc),
        grid=(batch, ngrp, seq // tc),
        in_specs=[
            pl.BlockSpec((1, tc, wide), lambda b, hg, ti: (b, ti, 4 * ngrp + hg)),
            pl.BlockSpec((1, tc, wide), lambda b, hg, ti: (b, ti, 5 * ngrp + hg)),
            pl.BlockSpec((1, tc, wide), lambda b, hg, ti: (b, ti, 6 * ngrp + hg)),
            pl.BlockSpec((1, tc, wide), lambda b, hg, ti: (b, ti, 7 * ngrp + hg)),
            pl.BlockSpec((1, tc, LANES), lambda b, hg, ti: (b, ti, 0)),
            pl.BlockSpec((nb, 8, LANES), lambda b, hg, ti: (hg, 0, 0)),
            pl.BlockSpec((nb, 8, LANES), lambda b, hg, ti: (hg, 0, 0)),
            pl.BlockSpec((LORA, wide), lambda b, hg, ti: (0, hg)),
            pl.BlockSpec((LORA, wide), lambda b, hg, ti: (0, hg)),
        ],
        out_specs=pl.BlockSpec((1, tc, wide), lambda b, hg, ti: (b, ti, hg)),
        out_shape=jax.ShapeDtypeStruct((batch, seq, HEAD_W), BF16),
        scratch_shapes=[
            pltpu.VMEM((nb, LANES, LANES), F32),
            pltpu.VMEM((4, nb, 1, LANES), F32),
        ],
        compiler_params=pltpu.CompilerParams(
            dimension_semantics=("parallel", "parallel", "arbitrary"),
            vmem_limit_bytes=VMEM_LIMIT),
        name="rwkv7",
    )(main, main, main, main, small, mu_rows, par_rows, w2, a2)


def _sgu_kernel(u_ref, v_ref, g_ref, lng_ref, lnb_ref, ws_ref, bs_ref, o_ref, *, rows):
    c = GMLP_CHUNK
    causal = (lax.broadcasted_iota(jnp.int32, (c, c), 0)
              >= lax.broadcasted_iota(jnp.int32, (c, c), 1))
    v = _gelu(v_ref[...].astype(F32))
    mu = jnp.mean(v, axis=-1, keepdims=True)
    vc = v - mu
    var = jnp.mean(vc * vc, axis=-1, keepdims=True)
    vn = (vc * lax.rsqrt(var + LN_EPS) * lng_ref[...] + lnb_ref[...]).astype(BF16)
    for grp in range(GMLP_GROUPS):
        cs = slice(grp * c, (grp + 1) * c)
        w = jnp.where(causal, ws_ref[grp], 0.0).astype(BF16)
        bias = bs_ref[:, grp:grp + 1]
        for ch in range(rows // c):
            rs = slice(ch * c, (ch + 1) * c)
            mixed = _dot(w, vn[rs, cs]) + bias
            u = _gelu(u_ref[rs, cs].astype(F32))
            o_ref[rs, cs] = (u * mixed * _silu(g_ref[rs, cs].astype(F32))).astype(o_ref.dtype)


def _spatial_gating(proj, ln_g, ln_b, w_s, b_s_t, *, rows=512):
    m = proj.shape[0]
    d = D_MODEL
    return pl.pallas_call(
        functools.partial(_sgu_kernel, rows=rows),
        grid=(m // rows,),
        in_specs=[
            pl.BlockSpec((rows, d), lambda i: (i, 0)),
            pl.BlockSpec((rows, d), lambda i: (i, 1)),
            pl.BlockSpec((rows, d), lambda i: (i, 2)),
            pl.BlockSpec((1, d), lambda i: (0, 0)),
            pl.BlockSpec((1, d), lambda i: (0, 0)),
            pl.BlockSpec((GMLP_GROUPS, GMLP_CHUNK, GMLP_CHUNK), lambda i: (0, 0, 0)),
            pl.BlockSpec((GMLP_CHUNK, GMLP_GROUPS), lambda i: (0, 0)),
        ],
        out_specs=pl.BlockSpec((rows, d), lambda i: (i, 0)),
        out_shape=jax.ShapeDtypeStruct((m, d), BF16),
        compiler_params=pltpu.CompilerParams(
            dimension_semantics=("parallel",), vmem_limit_bytes=VMEM_LIMIT),
        name="spatial_gating",
    )(proj, proj, proj, ln_g, ln_b, w_s, b_s_t)


def _tail_kernel(a0_ref, a1_ref, h_ref, p_ref, w0_ref, w1_ref, gp_ref, wp_ref, wg_ref, o_ref):
    y = _dot(a0_ref[...], w0_ref[...]) + _dot(a1_ref[...], w1_ref[...])
    ms = jnp.mean(y * y, axis=-1, keepdims=True)
    h1 = h_ref[...] + y * lax.rsqrt(ms + RMS_EPS) * gp_ref[...]
    gate = _dot(h1.astype(BF16), wg_ref[...])
    pp = _dot(p_ref[...].astype(BF16), wp_ref[...])
    o_ref[...] = h1 + pp * jax.nn.sigmoid(gate)


def _tail(act0, blk0, act1, blk1, h, p, w_out, g_post, w_proj, w_gate, *, tm=512):
    m, d = h.shape
    half = w_out.shape[0] // 2
    return pl.pallas_call(
        _tail_kernel,
        grid=(m // tm,),
        in_specs=[
            pl.BlockSpec((tm, half), lambda i: (i, blk0)),
            pl.BlockSpec((tm, half), lambda i: (i, blk1)),
            pl.BlockSpec((tm, d), lambda i: (i, 0)),
            pl.BlockSpec((tm, PLE_DIM), lambda i: (i, 0)),
            _resident((half, d), lambda i: (0, 0)),
            _resident((half, d), lambda i: (1, 0)),
            _resident((1, d), lambda i: (0, 0)),
            _resident((PLE_DIM, d), lambda i: (0, 0)),
            _resident((d, d), lambda i: (0, 0)),
        ],
        out_specs=pl.BlockSpec((tm, d), lambda i: (i, 0)),
        out_shape=jax.ShapeDtypeStruct((m, d), F32),
        compiler_params=pltpu.CompilerParams(
            dimension_semantics=("parallel",), vmem_limit_bytes=VMEM_LIMIT),
        name="tail",
    )(act0, act1, h, p, w_out, w_out, g_post, w_proj, w_gate)


def _per_block_rows(vecs):
    nblk = HEAD_W // LANES
    rows = [v.reshape(nblk, 1, LANES) for v in vecs]
    rows += [jnp.zeros((nblk, 1, LANES), F32)] * (8 - len(rows))
    return jnp.concatenate(rows, axis=1)


def kernel(x, p, norm_pre, norm_post, ab_w_in, fox_f_bias, rwkv_mu, rwkv_w0, rwkv_w2, rwkv_a0, rwkv_a2, rwkv_k_k, rwkv_k_a, rwkv_r_k, rwkv_ln_g, rwkv_ln_b, ab_w_out, c_w_in, c_ln_g, c_ln_b, c_w_s, c_b_s, c_w_out, ple_w_proj, ple_w_gate):
    batch, seq, d = x.shape
    m = batch * seq
    w = HEAD_W
    h = x.reshape(m, d)

    w_in = ab_w_in[0]
    o_f = 3 * w
    o_ga = o_f + HEADS
    o_sh = o_ga + w
    o_lo = o_sh + 3 * w
    o_gb = o_lo + 2 * LORA
    w_bf = w_in.astype(BF16)
    w_main = jnp.concatenate([w_bf[:, :o_f], w_bf[:, o_ga:o_lo], w_bf[:, o_gb:]], axis=1)
    w_small = jnp.concatenate(
        [w_bf[:, o_lo:o_gb], w_bf[:, o_f:o_ga], jnp.zeros((d, LANES - HEADS), BF16)], axis=1)
    main, small = _norm_matmul(h, norm_pre[0:1], w_main, w_small)
    main3 = main.reshape(batch, seq, 8 * w)
    small3 = small.reshape(batch, seq, 2 * LANES)

    f_bias = jnp.concatenate([fox_f_bias[0], jnp.zeros((LANES - HEADS,), F32)]).reshape(1, LANES)
    fox_bias = _fox_bias(small3, f_bias, batch, seq)
    act_a = _fox_attention(main3, fox_bias, batch, seq)

    mu = rwkv_mu[0]
    mu_rows = _per_block_rows([mu[:w], mu[w:2 * w], mu[2 * w:3 * w]])
    mu_rows = mu_rows.at[:, 3, :].set(jnp.broadcast_to(mu[3 * w:], (w // LANES, LANES)))
    par_rows = _per_block_rows([rwkv_w0[0], rwkv_a0[0], rwkv_k_k[0], rwkv_k_a[0],
                                rwkv_r_k[0].reshape(w), rwkv_ln_g[0], rwkv_ln_b[0]])
    act_b = _rwkv(main3, small3, mu_rows, par_rows, rwkv_w2[0].astype(BF16),
                  rwkv_a2[0].astype(BF16), batch, seq)

    h = _tail(act_a.reshape(m, w), 0, act_b.reshape(m, w), 0, h, p[0].reshape(m, PLE_DIM),
              ab_w_out[0].astype(BF16), norm_post[0:1], ple_w_proj[0].astype(BF16),
              ple_w_gate[0].astype(BF16))

    proj = _norm_matmul(h, norm_pre[1:2], c_w_in[0].astype(BF16))[0]
    act_c = _spatial_gating(proj, c_ln_g[0:1], c_ln_b[0:1], c_w_s[0], c_b_s[0].T)
    h = _tail(act_c, 0, act_c, 1, h, p[1].reshape(m, PLE_DIM),
              c_w_out[0].astype(BF16), norm_post[1:2], ple_w_proj[1].astype(BF16),
              ple_w_gate[1].astype(BF16))
    return h.reshape(batch, seq, d)
```

```python
import functools

import jax
import jax.numpy as jnp
from jax import lax
from jax.experimental import pallas as pl
from jax.experimental.pallas import tpu as pltpu

F32 = jnp.float32
BF16 = jnp.bfloat16

D_MODEL = 2048
PLE_DIM = 256
HEADS = 16
HEAD_DIM = 64
HEAD_W = HEADS * HEAD_DIM
LORA = 64
GMLP_GROUPS = 16
GMLP_CHUNK = 128
RMS_EPS = 1e-6
LN_EPS = 1e-5
RWKV_GN_EPS = 64e-5
LANES = 128
HEADS_PER_BLOCK = LANES // HEAD_DIM
RWKV_CHUNK = 64
VMEM_LIMIT = 56 * 1024 * 1024

NT_DIMS = (((1,), (1,)), ((), ()))
TN_DIMS = (((0,), (0,)), ((), ()))


def _dot(a, b):
    return jnp.dot(a, b, preferred_element_type=F32)


def _split(x, terms):
    parts = []
    for _ in range(terms - 1):
        part = x.astype(BF16)
        parts.append(part)
        x = x - part.astype(F32)
    parts.append(x.astype(BF16))
    return parts


def _dot_exact_lhs(a_bf16, x, terms=3):
    return sum(_dot(a_bf16, part) for part in _split(x, terms))


def _resident(block_shape, index_map):
    return pl.BlockSpec(block_shape, index_map, pipeline_mode=pl.Buffered(1))


def _silu(x):
    return x * jax.nn.sigmoid(x)


def _gelu(x):
    return 0.5 * x * (1.0 + lax.erf(x * (0.5 ** 0.5)))


def _norm_mm_kernel(x_ref, g_ref, w_ref, *rest, has_small):
    if has_small:
        ws_ref, o_ref, os_ref, xn_ref = rest
    else:
        o_ref, xn_ref = rest

    @pl.when(pl.program_id(1) == 0)
    def _():
        x = x_ref[...]
        ms = jnp.mean(x * x, axis=-1, keepdims=True)
        xn = (x * lax.rsqrt(ms + RMS_EPS) * g_ref[...]).astype(BF16)
        xn_ref[...] = xn
        if has_small:
            os_ref[...] = _dot(xn, ws_ref[...])

    o_ref[...] = _dot(xn_ref[...], w_ref[...]).astype(o_ref.dtype)


def _norm_matmul(x, g, w, w_small=None, *, tm=1024, tn=2048):
    m, d = x.shape
    n = w.shape[1]
    has_small = w_small is not None
    in_specs = [
        pl.BlockSpec((tm, d), lambda i, j: (i, 0)),
        _resident((1, d), lambda i, j: (0, 0)),
        pl.BlockSpec((d, tn), lambda i, j: (0, j)),
    ]
    out_specs = [pl.BlockSpec((tm, tn), lambda i, j: (i, j))]
    out_shape = [jax.ShapeDtypeStruct((m, n), BF16)]
    args = [x, g, w]
    if has_small:
        ns = w_small.shape[1]
        in_specs.append(_resident((d, ns), lambda i, j: (0, 0)))
        out_specs.append(pl.BlockSpec((tm, ns), lambda i, j: (i, 0)))
        out_shape.append(jax.ShapeDtypeStruct((m, ns), F32))
        args.append(w_small)
    return pl.pallas_call(
        functools.partial(_norm_mm_kernel, has_small=has_small),
        grid=(m // tm, n // tn),
        in_specs=in_specs,
        out_specs=out_specs,
        out_shape=out_shape,
        scratch_shapes=[pltpu.VMEM((tm, d), BF16)],
        compiler_params=pltpu.CompilerParams(
            dimension_semantics=("parallel", "arbitrary"), vmem_limit_bytes=VMEM_LIMIT),
        name="norm_matmul_small" if has_small else "norm_matmul",
    )(*args)


CUMSUM_BLOCK = 256
BIAS_TERMS = 3


def _bias_lane(head):
    return jnp.where(head % HEADS_PER_BLOCK == 0, HEAD_DIM, 0)


def _fox_bias_kernel(f_ref, b_ref, o_ref):
    t = f_ref.shape[1]
    row = lax.broadcasted_iota(jnp.int32, (CUMSUM_BLOCK, CUMSUM_BLOCK), 0)
    col = lax.broadcasted_iota(jnp.int32, (CUMSUM_BLOCK, CUMSUM_BLOCK), 1)
    tri = (row >= col).astype(BF16)
    packed_lane = lax.broadcasted_iota(jnp.int32, (LANES, HEAD_W), 0)
    lane = lax.broadcasted_iota(jnp.int32, (LANES, HEAD_W), 1)
    head = packed_lane % HEADS
    term = packed_lane // HEADS
    place = ((term < BIAS_TERMS) & (lane // LANES == head // HEADS_PER_BLOCK)
             & (lane % LANES == _bias_lane(head) + term)).astype(BF16)
    is_head = lax.broadcasted_iota(jnp.int32, (CUMSUM_BLOCK, LANES), 1) < HEADS
    carry = jnp.zeros((1, LANES), F32)
    for blk in range(t // CUMSUM_BLOCK):
        sl = slice(blk * CUMSUM_BLOCK, (blk + 1) * CUMSUM_BLOCK)
        log_f = jax.nn.log_sigmoid(f_ref[0, sl, :] + b_ref[...])
        c = _dot_exact_lhs(tri, log_f) + carry
        carry = c[CUMSUM_BLOCK - 1:CUMSUM_BLOCK, :]
        terms = _split(jnp.where(is_head, c * (-(HEAD_DIM ** 0.5)), 0.0), BIAS_TERMS)
        packed = terms[0].astype(F32)
        for j in range(1, BIAS_TERMS):
            packed = packed + pltpu.roll(terms[j].astype(F32), j * HEADS, 1)
        o_ref[0, sl, :] = _dot(packed.astype(BF16), place).astype(o_ref.dtype)


def _fox_bias(small, f_bias_padded, batch, seq):
    return pl.pallas_call(
        _fox_bias_kernel,
        grid=(batch,),
        in_specs=[
            pl.BlockSpec((1, seq, LANES), lambda b: (b, 0, 1)),
            pl.BlockSpec((1, LANES), lambda b: (0, 0)),
        ],
        out_specs=pl.BlockSpec((1, seq, HEAD_W), lambda b: (b, 0, 0)),
        out_shape=jax.ShapeDtypeStruct((batch, seq, HEAD_W), BF16),
        compiler_params=pltpu.CompilerParams(
            dimension_semantics=("parallel",), vmem_limit_bytes=VMEM_LIMIT),
        name="fox_bias",
    )(small, f_bias_padded)


MASK_VALUE = -1e30
LOG2E = 1.4426950408889634


VT_ROWS = HEAD_DIM + 16
FOX_BLOCKS = 2


def _fox_attn_kernel(q_ref, k_ref, v_ref, bias_ref, g_ref, o_ref,
                     kaug_scr, vt_scr, *, blk):
    qi = pl.program_id(2)
    seq = k_ref.shape[1]
    nh = FOX_BLOCKS * HEADS_PER_BLOCK

    @pl.when(qi == 0)
    def _():
        lane = lax.broadcasted_iota(jnp.int32, (seq, LANES), 1)
        ones_row = (lax.broadcasted_iota(jnp.int32, (VT_ROWS - HEAD_DIM, blk), 0) == 0).astype(BF16)
        for p in range(FOX_BLOCKS):
            ls = slice(p * LANES, (p + 1) * LANES)
            k = k_ref[0, :, ls]
            bias = bias_ref[0, :, ls]
            kaug_scr[HEADS_PER_BLOCK * p] = jnp.where(lane < HEAD_DIM, k, bias)
            kaug_scr[HEADS_PER_BLOCK * p + 1] = jnp.where(lane >= HEAD_DIM, k, bias)
            for j in range(seq // blk):
                v_t = v_ref[0, j * blk:(j + 1) * blk, ls].T
                for h in range(HEADS_PER_BLOCK):
                    vt_scr[j, HEADS_PER_BLOCK * p + h] = jnp.concatenate(
                        [v_t[h * HEAD_DIM:(h + 1) * HEAD_DIM], ones_row], axis=0)

    lane = lax.broadcasted_iota(jnp.int32, (blk, LANES), 1)
    ones0 = ((lane >= HEAD_DIM) & (lane < HEAD_DIM + BIAS_TERMS)).astype(BF16)
    ones1 = (lane < BIAS_TERMS).astype(BF16)
    q_aug = []
    for p in range(FOX_BLOCKS):
        q = q_ref[0, :, p * LANES:(p + 1) * LANES]
        q_aug += [jnp.where(lane < HEAD_DIM, q, ones0), jnp.where(lane >= HEAD_DIM, q, ones1)]
    q_aug = jnp.stack(q_aug)
    kappa = (HEAD_DIM ** -0.5) * LOG2E

    half = blk // 2

    def scores(step):
        (k0, k1), (q0, q1) = step
        return _bdot_nt(kaug_scr[:, k0:k1, :], q_aug[:, q0:q1, :])

    def softmax_step(s_t, m_prev, masked):
        if masked:
            key = lax.broadcasted_iota(jnp.int32, s_t.shape[1:], 0)
            qry = lax.broadcasted_iota(jnp.int32, s_t.shape[1:], 1)
            s_t = jnp.where(key <= qry, s_t, MASK_VALUE)
        m_new = jnp.maximum(m_prev, jnp.max(s_t, axis=1, keepdims=True))
        alpha = jnp.exp2((m_prev - m_new) * kappa)
        return jnp.exp2((s_t - m_new) * kappa).astype(BF16), alpha, m_new

    def attend(last):
        steps = [((j * blk, (j + 1) * blk), (0, blk)) for j in range(last)]
        steps += [((last * blk, last * blk + half), (0, blk)),
                  ((last * blk + half, (last + 1) * blk), (half, blk))]
        m = jnp.full((nh, 1, blk), MASK_VALUE, F32)
        acc = jnp.zeros((nh, VT_ROWS, blk), F32)

        def accumulate(acc, step, p_t, alpha):
            (k0, k1), (q0, q1) = step
            v_t = vt_scr[k0 // blk][:, :, k0 % blk:k0 % blk + (k1 - k0)]
            new = alpha * acc[:, :, q0:q1] + _bdot(v_t, p_t)
            return new if q0 == 0 else jnp.concatenate([acc[:, :, :q0], new], axis=2)

        s_next = scores(steps[0])
        pending = None
        for i, step in enumerate(steps):
            s_cur = s_next
            if i + 1 < len(steps):
                s_next = scores(steps[i + 1])
            q0, q1 = step[1]
            p_t, alpha, m_part = softmax_step(s_cur, m[:, :, q0:q1], masked=(i >= last))
            m = m_part if q0 == 0 else jnp.concatenate([m[:, :, :q0], m_part], axis=2)
            if pending is not None:
                acc = accumulate(acc, *pending)
            pending = (step, p_t, alpha)
        acc = accumulate(acc, *pending)
        o_t = acc[:, :HEAD_DIM, :] * (1.0 / acc[:, HEAD_DIM:HEAD_DIM + 1, :])
        o = o_t.reshape(nh * HEAD_DIM, blk).T
        o_ref[0] = (o * _silu(g_ref[0].astype(F32))).astype(o_ref.dtype)

    for last in range(seq // blk):
        pl.when(qi == last)(functools.partial(attend, last))


def _fox_attention(main, bias, batch, seq, *, blk=512):
    wide = FOX_BLOCKS * LANES
    nblk = HEAD_W // wide
    nq = seq // blk
    return pl.pallas_call(
        functools.partial(_fox_attn_kernel, blk=blk),
        grid=(batch, nblk, nq),
        in_specs=[
            pl.BlockSpec((1, blk, wide), lambda b, hp, qi: (b, qi, hp)),
            pl.BlockSpec((1, seq, wide), lambda b, hp, qi: (b, 0, nblk + hp)),
            pl.BlockSpec((1, seq, wide), lambda b, hp, qi: (b, 0, 2 * nblk + hp)),
            pl.BlockSpec((1, seq, wide), lambda b, hp, qi: (b, 0, hp)),
            pl.BlockSpec((1, blk, wide), lambda b, hp, qi: (b, qi, 3 * nblk + hp)),
        ],
        out_specs=pl.BlockSpec((1, blk, wide), lambda b, hp, qi: (b, qi, hp)),
        out_shape=jax.ShapeDtypeStruct((batch, seq, HEAD_W), BF16),
        scratch_shapes=[
            pltpu.VMEM((FOX_BLOCKS * HEADS_PER_BLOCK, seq, LANES), BF16),
            pltpu.VMEM((seq // blk, FOX_BLOCKS * HEADS_PER_BLOCK, VT_ROWS, blk), BF16),
        ],
        compiler_params=pltpu.CompilerParams(
            dimension_semantics=("parallel", "parallel", "arbitrary"),
            vmem_limit_bytes=VMEM_LIMIT),
        name="fox_attention",
    )(main, main, main, bias, main)


def _head_sum(x, terms):
    width = x.shape[-1]
    row = lax.broadcasted_iota(jnp.int32, (width, width), 0)
    col = lax.broadcasted_iota(jnp.int32, (width, width), 1)
    same_head = (row // HEAD_DIM == col // HEAD_DIM).astype(BF16)
    return sum(_dot(part, same_head) for part in _split(x, terms))


def _stack_heads(x):
    head = lax.broadcasted_iota(jnp.int32, x.shape, x.ndim - 1) // HEAD_DIM
    zero = jnp.zeros_like(x)
    return jnp.concatenate(
        [jnp.where(head == h, x, zero) for h in range(x.shape[-1] // HEAD_DIM)], axis=x.ndim - 2)


def _bdot(a, b):
    return lax.dot_general(a, b, (((2,), (1,)), ((0,), (0,))), preferred_element_type=F32)


def _bdot_nt(a, b):
    return lax.dot_general(a, b, (((2,), (2,)), ((0,), (0,))), preferred_element_type=F32)


def _bdot_tn(a, b):
    return lax.dot_general(a, b, (((1,), (1,)), ((0,), (0,))), preferred_element_type=F32)


def _unit_lower_inverse(l_strict, order):
    c, width = l_strict.shape[1:]
    eye = (lax.broadcasted_iota(jnp.int32, (c, width), 0)
           == lax.broadcasted_iota(jnp.int32, (c, width), 1) % HEAD_DIM).astype(F32)
    inv = eye + l_strict
    power = l_strict.astype(BF16)
    power = _bdot(power, _stack_heads(power)).astype(BF16)
    last = order.bit_length() - 2
    for k in range(1, last):
        both = _bdot(jnp.concatenate([inv.astype(BF16), power], axis=1), _stack_heads(power))
        inv = inv + both[:, :c]
        power = both[:, c:].astype(BF16)
    return inv + _bdot(inv.astype(BF16), _stack_heads(power))


RWKV_WIDTH = 128
RWKV_BLOCKS = 8


def _rwkv_kernel(r_ref, k_ref, v_ref, g_ref, wa_ref, mu_ref, par_ref, w2_ref, a2_ref,
                 o_ref, state_scr, prev_scr, *, tc):
    ti = pl.program_id(2)
    nb = RWKV_BLOCKS
    wd = RWKV_WIDTH
    c = RWKV_CHUNK
    nc = tc // c

    @pl.when(ti == 0)
    def _():
        state_scr[...] = jnp.zeros(state_scr.shape, F32)
        prev_scr[...] = jnp.zeros(prev_scr.shape, F32)

    def blocks(ref, dtype=None):
        x = jnp.stack([ref[0, :, i * wd:(i + 1) * wd] for i in range(nb)])
        return x if dtype is None else x.astype(dtype)

    def shift_lerp(x, prev_row, mu):
        rolled = pltpu.roll(x, 1, x.ndim - 2)
        first = lax.broadcasted_iota(jnp.int32, x.shape, x.ndim - 2) == 0
        return x + (jnp.where(first, prev_row, rolled) - x) * mu

    def head_sum(x, terms):
        return _head_sum(x.reshape(nb * tc, wd), terms).reshape(nb, tc, wd)

    w0, a0, k_k, k_a, r_k, ln_g, ln_b = (par_ref[:, i:i + 1, :] for i in range(7))
    r_in, k_in, v_in = blocks(r_ref, F32), blocks(k_ref, F32), blocks(v_ref, F32)
    wa_in = wa_ref[0]
    r = shift_lerp(r_in, prev_scr[0], mu_ref[:, 0:1, :])
    k = shift_lerp(k_in, prev_scr[1], mu_ref[:, 1:2, :])
    v = shift_lerp(v_in, prev_scr[2], mu_ref[:, 2:3, :])
    wa = shift_lerp(wa_in, prev_scr[3, 0, :, :LANES], mu_ref[0, 3:4, :LANES])
    prev_scr[0] = r_in[:, tc - 1:tc, :]
    prev_scr[1] = k_in[:, tc - 1:tc, :]
    prev_scr[2] = v_in[:, tc - 1:tc, :]
    prev_scr[3, 0, :, :LANES] = wa_in[tc - 1:tc, :]

    def per_block(x):
        return jnp.stack([x[:, i * wd:(i + 1) * wd] for i in range(nb)])

    w_lo = jnp.tanh(wa[:, :LORA]).astype(BF16)
    a_lo = wa[:, LORA:].astype(BF16)
    w_raw = -jax.nn.softplus(-(w0 + per_block(_dot(w_lo, w2_ref[...])))) - 0.5
    logw = -jnp.exp(w_raw)
    a = jax.nn.sigmoid(a0 + per_block(_dot(a_lo, a2_ref[...])))
    kk = k * k_k
    kk = kk * lax.rsqrt(jnp.maximum(head_sum(kk * kk, 1), 1e-24))
    k = k * (1.0 + (a - 1.0) * k_a)
    bonus = head_sum(r * k * r_k, 1) * v

    row = lax.broadcasted_iota(jnp.int32, (tc, tc), 0)
    col = lax.broadcasted_iota(jnp.int32, (tc, tc), 1)
    tri_blk = ((row >= col) & (row // c == col // c)).astype(BF16)
    cum = jnp.stack([_dot_exact_lhs(tri_blk, logw[b], 2) for b in range(nb)])

    trow = lax.broadcasted_iota(jnp.int32, (c, wd), 0)
    tcol = lax.broadcasted_iota(jnp.int32, (c, wd), 1) % HEAD_DIM
    lower_incl = trow >= tcol
    lower_strict = trow > tcol

    def chunked(x):
        return x.reshape(nb * nc, c, wd)

    cum3 = chunked(cum)
    cum_end = cum3[:, c - 1:c, :]
    e_neg = jnp.exp(-cum3)
    e_end = jnp.exp(cum_end - cum3)
    akk = chunked(a * kk)
    k3 = chunked(k)
    r_t = chunked(r) * jnp.exp(cum3)
    p_t = (-chunked(kk) * jnp.exp(cum3 - chunked(logw))).astype(BF16)
    p_s = _stack_heads(p_t)
    z_s = _stack_heads((akk * e_neg).astype(BF16))
    k_s = _stack_heads((k3 * e_neg).astype(BF16))
    zbar_s = _stack_heads((akk * e_end).astype(BF16))
    kbar_s = _stack_heads((k3 * e_end).astype(BF16))
    v_s = _stack_heads(chunked(v).astype(BF16))
    pr = jnp.concatenate([p_t, r_t.astype(BF16)], axis=1)
    zk = jnp.concatenate([z_s, k_s], axis=1)
    amat = _bdot_nt(pr, zk)
    l_pz = jnp.where(lower_strict, amat[:, :c, :wd], 0.0)
    l_pk = jnp.where(lower_strict, amat[:, :c, wd:], 0.0).astype(BF16)
    l_rz = jnp.where(lower_incl, amat[:, c:, :wd], 0.0).astype(BF16)
    l_rk = jnp.where(lower_incl, amat[:, c:, wd:], 0.0).astype(BF16)
    inv = _unit_lower_inverse(l_pz, c).astype(BF16)
    lv = _bdot(l_pk, v_s).astype(BF16)
    tp_ui = _bdot(inv, jnp.concatenate([p_s, _stack_heads(lv)], axis=2)).astype(BF16)
    tp_ui_s = jnp.concatenate(
        [_stack_heads(tp_ui[:, :, :wd]), _stack_heads(tp_ui[:, :, wd:])], axis=2)
    rz = _bdot(l_rz, tp_ui_s)
    qeff = (r_t + rz[:, :, :wd]).astype(BF16)
    yi = rz[:, :, wd:] + _bdot(l_rk, v_s)
    mz = _bdot_tn(tp_ui_s, zbar_s)
    m_low = mz[:, :wd].astype(BF16)
    b_mat = mz[:, wd:] + _bdot_tn(v_s, kbar_s)
    g_end = jnp.exp(cum_end)

    def of_chunk(x, ci):
        return x.reshape((nb, nc) + x.shape[1:])[:, ci]

    s = state_scr[...]
    ys = []
    for ci in range(nc):
        s_b = s.astype(BF16)
        ys.append(_bdot_nt(of_chunk(qeff, ci), s_b) + of_chunk(yi, ci))
        s = s * of_chunk(g_end, ci) + _bdot(s_b, of_chunk(m_low, ci)) + of_chunk(b_mat, ci)
    state_scr[...] = s

    y = jnp.concatenate(ys, axis=1)
    mean = head_sum(y, 2) * (1.0 / HEAD_DIM)
    yc = y - mean
    var = head_sum(yc * yc, 1) * (1.0 / HEAD_DIM)
    y = yc * lax.rsqrt(var + RWKV_GN_EPS) * ln_g + ln_b + bonus
    y = y * _silu(blocks(g_ref, F32))
    for b in range(nb):
        o_ref[0, :, b * wd:(b + 1) * wd] = y[b].astype(o_ref.dtype)


def _rwkv(main, small, mu_rows, par_rows, w2, a2, batch, seq, *, tc=128):
    nb = RWKV_BLOCKS
    wide = nb * RWKV_WIDTH
    ngrp = HEAD_W // wide
    return pl.pallas_call(
        functools.partial(_rwkv_kernel, tc=tc),
        grid=(batch, ngrp, seq // tc),
        in_specs=[
            pl.BlockSpec((1, tc, wide), lambda b, hg, ti: (b, ti, 4 * ngrp + hg)),
            pl.BlockSpec((1, tc, wide), lambda b, hg, ti: (b, ti, 5 * ngrp + hg)),
            pl.BlockSpec((1, tc, wide), lambda b, hg, ti: (b, ti, 6 * ngrp + hg)),
            pl.BlockSpec((1, tc, wide), lambda b, hg, ti: (b, ti, 7 * ngrp + hg)),
            pl.BlockSpec((1, tc, LANES), lambda b, hg, ti: (b, ti, 0)),
            pl.BlockSpec((nb, 8, RWKV_WIDTH), lambda b, hg, ti: (hg, 0, 0)),
            pl.BlockSpec((nb, 8, RWKV_WIDTH), lambda b, hg, ti: (hg, 0, 0)),
            pl.BlockSpec((LORA, wide), lambda b, hg, ti: (0, hg)),
            pl.BlockSpec((LORA, wide), lambda b, hg, ti: (0, hg)),
        ],
        out_specs=pl.BlockSpec((1, tc, wide), lambda b, hg, ti: (b, ti, hg)),
        out_shape=jax.ShapeDtypeStruct((batch, seq, HEAD_W), BF16),
        scratch_shapes=[
            pltpu.VMEM((nb, RWKV_WIDTH, RWKV_WIDTH), F32),
            pltpu.VMEM((4, nb, 1, RWKV_WIDTH), F32),
        ],
        compiler_params=pltpu.CompilerParams(
            dimension_semantics=("parallel", "parallel", "arbitrary"),
            vmem_limit_bytes=VMEM_LIMIT),
        name="rwkv7",
    )(main, main, main, main, small, mu_rows, par_rows, w2, a2)


def _sgu_kernel(u_ref, v_ref, g_ref, lng_ref, lnb_ref, ws_ref, bs_ref, o_ref, *, rows):
    c = GMLP_CHUNK
    causal = (lax.broadcasted_iota(jnp.int32, (c, c), 0)
              >= lax.broadcasted_iota(jnp.int32, (c, c), 1))
    v = _gelu(v_ref[...].astype(F32))
    mu = jnp.mean(v, axis=-1, keepdims=True)
    vc = v - mu
    var = jnp.mean(vc * vc, axis=-1, keepdims=True)
    vn = (vc * lax.rsqrt(var + LN_EPS) * lng_ref[...] + lnb_ref[...]).astype(BF16)
    for grp in range(GMLP_GROUPS):
        cs = slice(grp * c, (grp + 1) * c)
        w = jnp.where(causal, ws_ref[grp], 0.0).astype(BF16)
        bias = bs_ref[:, grp:grp + 1]
        for ch in range(rows // c):
            rs = slice(ch * c, (ch + 1) * c)
            mixed = _dot(w, vn[rs, cs]) + bias
            u = _gelu(u_ref[rs, cs].astype(F32))
            o_ref[rs, cs] = (u * mixed * _silu(g_ref[rs, cs].astype(F32))).astype(o_ref.dtype)


def _spatial_gating(proj, ln_g, ln_b, w_s, b_s_t, *, rows=512):
    m = proj.shape[0]
    d = D_MODEL
    return pl.pallas_call(
        functools.partial(_sgu_kernel, rows=rows),
        grid=(m // rows,),
        in_specs=[
            pl.BlockSpec((rows, d), lambda i: (i, 0)),
            pl.BlockSpec((rows, d), lambda i: (i, 1)),
            pl.BlockSpec((rows, d), lambda i: (i, 2)),
            pl.BlockSpec((1, d), lambda i: (0, 0)),
            pl.BlockSpec((1, d), lambda i: (0, 0)),
            pl.BlockSpec((GMLP_GROUPS, GMLP_CHUNK, GMLP_CHUNK), lambda i: (0, 0, 0)),
            pl.BlockSpec((GMLP_CHUNK, GMLP_GROUPS), lambda i: (0, 0)),
        ],
        out_specs=pl.BlockSpec((rows, d), lambda i: (i, 0)),
        out_shape=jax.ShapeDtypeStruct((m, d), BF16),
        compiler_params=pltpu.CompilerParams(
            dimension_semantics=("parallel",), vmem_limit_bytes=VMEM_LIMIT),
        name="spatial_gating",
    )(proj, proj, proj, ln_g, ln_b, w_s, b_s_t)


def _tail_kernel(a0_ref, a1_ref, h_ref, p_ref, w0_ref, w1_ref, gp_ref, wp_ref, wg_ref, o_ref):
    y = _dot(a0_ref[...], w0_ref[...]) + _dot(a1_ref[...], w1_ref[...])
    ms = jnp.mean(y * y, axis=-1, keepdims=True)
    h1 = h_ref[...] + y * lax.rsqrt(ms + RMS_EPS) * gp_ref[...]
    gate = _dot(h1.astype(BF16), wg_ref[...])
    pp = _dot(p_ref[...].astype(BF16), wp_ref[...])
    o_ref[...] = h1 + pp * jax.nn.sigmoid(gate)


def _tail(act0, blk0, act1, blk1, h, p_all, layer, w_out, g_post, w_proj, w_gate, *, tm=512):
    m, d = h.shape
    p_row0 = layer * (m // tm)
    half = w_out.shape[0] // 2
    return pl.pallas_call(
        _tail_kernel,
        grid=(m // tm,),
        in_specs=[
            pl.BlockSpec((tm, half), lambda i: (i, blk0)),
            pl.BlockSpec((tm, half), lambda i: (i, blk1)),
            pl.BlockSpec((tm, d), lambda i: (i, 0)),
            pl.BlockSpec((tm, PLE_DIM), lambda i: (p_row0 + i, 0)),
            _resident((half, d), lambda i: (0, 0)),
            _resident((half, d), lambda i: (1, 0)),
            _resident((1, d), lambda i: (0, 0)),
            _resident((PLE_DIM, d), lambda i: (0, 0)),
            _resident((d, d), lambda i: (0, 0)),
        ],
        out_specs=pl.BlockSpec((tm, d), lambda i: (i, 0)),
        out_shape=jax.ShapeDtypeStruct((m, d), F32),
        compiler_params=pltpu.CompilerParams(
            dimension_semantics=("parallel",), vmem_limit_bytes=VMEM_LIMIT),
        name="tail",
    )(act0, act1, h, p_all, w_out, w_out, g_post, w_proj, w_gate)


def _per_block_rows(vecs):
    nblk = HEAD_W // RWKV_WIDTH
    rows = [v.reshape(nblk, 1, RWKV_WIDTH) for v in vecs]
    rows += [jnp.zeros((nblk, 1, RWKV_WIDTH), F32)] * (8 - len(rows))
    return jnp.concatenate(rows, axis=1)


def kernel(x, p, norm_pre, norm_post, ab_w_in, fox_f_bias, rwkv_mu, rwkv_w0, rwkv_w2, rwkv_a0, rwkv_a2, rwkv_k_k, rwkv_k_a, rwkv_r_k, rwkv_ln_g, rwkv_ln_b, ab_w_out, c_w_in, c_ln_g, c_ln_b, c_w_s, c_b_s, c_w_out, ple_w_proj, ple_w_gate):
    batch, seq, d = x.shape
    m = batch * seq
    w = HEAD_W
    h = x.reshape(m, d)

    w_in = ab_w_in[0]
    o_f = 3 * w
    o_ga = o_f + HEADS
    o_sh = o_ga + w
    o_lo = o_sh + 3 * w
    o_gb = o_lo + 2 * LORA
    w_bf = w_in.astype(BF16)
    w_main = jnp.concatenate([w_bf[:, :o_f], w_bf[:, o_ga:o_lo], w_bf[:, o_gb:]], axis=1)
    w_small = jnp.concatenate(
        [w_bf[:, o_lo:o_gb], w_bf[:, o_f:o_ga], jnp.zeros((d, LANES - HEADS), BF16)], axis=1)
    main, small = _norm_matmul(h, norm_pre[0:1], w_main, w_small)
    main3 = main.reshape(batch, seq, 8 * w)
    small3 = small.reshape(batch, seq, 2 * LANES)

    f_bias = jnp.concatenate([fox_f_bias[0], jnp.zeros((LANES - HEADS,), F32)]).reshape(1, LANES)
    fox_bias = _fox_bias(small3, f_bias, batch, seq)
    act_a = _fox_attention(main3, fox_bias, batch, seq)

    mu = rwkv_mu[0]
    mu_rows = _per_block_rows([mu[:w], mu[w:2 * w], mu[2 * w:3 * w]])
    mu_rows = mu_rows.at[:, 3, :2 * LORA].set(mu[3 * w:])
    par_rows = _per_block_rows([rwkv_w0[0], rwkv_a0[0], rwkv_k_k[0], rwkv_k_a[0],
                                rwkv_r_k[0].reshape(w), rwkv_ln_g[0], rwkv_ln_b[0]])
    act_b = _rwkv(main3, small3, mu_rows, par_rows, rwkv_w2[0].astype(BF16),
                  rwkv_a2[0].astype(BF16), batch, seq)

    p_all = p.reshape(p.shape[0] * m, PLE_DIM)
    h = _tail(act_a.reshape(m, w), 0, act_b.reshape(m, w), 0, h, p_all, 0,
              ab_w_out[0].astype(BF16), norm_post[0:1], ple_w_proj[0].astype(BF16),
              ple_w_gate[0].astype(BF16))

    proj = _norm_matmul(h, norm_pre[1:2], c_w_in[0].astype(BF16))[0]
    act_c = _spatial_gating(proj, c_ln_g[0:1], c_ln_b[0:1], c_w_s[0], c_b_s[0].T)
    h = _tail(act_c, 0, act_c, 1, h, p_all, 1,
              c_w_out[0].astype(BF16), norm_post[1:2], ple_w_proj[1].astype(BF16),
              ple_w_gate[1].astype(BF16))
    return h.reshape(batch, seq, d)
```

```python
import functools

import jax
import jax.numpy as jnp
from jax import lax
from jax.experimental import pallas as pl
from jax.experimental.pallas import tpu as pltpu

F32 = jnp.float32
BF16 = jnp.bfloat16

D_MODEL = 2048
PLE_DIM = 256
HEADS = 16
HEAD_DIM = 64
HEAD_W = HEADS * HEAD_DIM
LORA = 64
GMLP_GROUPS = 16
GMLP_CHUNK = 128
RMS_EPS = 1e-6
LN_EPS = 1e-5
RWKV_GN_EPS = 64e-5
LANES = 128
HEADS_PER_BLOCK = LANES // HEAD_DIM
RWKV_CHUNK = 64
VMEM_LIMIT = 56 * 1024 * 1024


def _dot(a, b):
    return jnp.dot(a, b, preferred_element_type=F32)


def _split(x, terms):
    parts = []
    for _ in range(terms - 1):
        part = x.astype(BF16)
        parts.append(part)
        x = x - part.astype(F32)
    parts.append(x.astype(BF16))
    return parts


def _dot_exact_lhs(a_bf16, x, terms=3):
    return sum(_dot(a_bf16, part) for part in _split(x, terms))


def _resident(block_shape, index_map):
    return pl.BlockSpec(block_shape, index_map, pipeline_mode=pl.Buffered(1))


def _silu(x):
    return x * jax.nn.sigmoid(x)


def _gelu(x):
    return 0.5 * x * (1.0 + lax.erf(x * (0.5 ** 0.5)))


def _norm_mm_kernel(x_ref, g_ref, w_ref, *rest, has_small):
    if has_small:
        ws_ref, o_ref, os_ref, xn_ref = rest
    else:
        o_ref, xn_ref = rest

    @pl.when(pl.program_id(1) == 0)
    def _():
        x = x_ref[...]
        ms = jnp.mean(x * x, axis=-1, keepdims=True)
        xn = (x * lax.rsqrt(ms + RMS_EPS) * g_ref[...]).astype(BF16)
        xn_ref[...] = xn
        if has_small:
            os_ref[...] = _dot(xn, ws_ref[...])

    o_ref[...] = _dot(xn_ref[...], w_ref[...]).astype(o_ref.dtype)


def _norm_matmul(x, g, w, w_small=None, *, tm=1024, tn=2048):
    m, d = x.shape
    n = w.shape[1]
    has_small = w_small is not None
    in_specs = [
        pl.BlockSpec((tm, d), lambda i, j: (i, 0)),
        _resident((1, d), lambda i, j: (0, 0)),
        pl.BlockSpec((d, tn), lambda i, j: (0, j)),
    ]
    out_specs = [pl.BlockSpec((tm, tn), lambda i, j: (i, j))]
    out_shape = [jax.ShapeDtypeStruct((m, n), BF16)]
    args = [x, g, w]
    if has_small:
        ns = w_small.shape[1]
        in_specs.append(_resident((d, ns), lambda i, j: (0, 0)))
        out_specs.append(pl.BlockSpec((tm, ns), lambda i, j: (i, 0)))
        out_shape.append(jax.ShapeDtypeStruct((m, ns), F32))
        args.append(w_small)
    return pl.pallas_call(
        functools.partial(_norm_mm_kernel, has_small=has_small),
        grid=(m // tm, n // tn),
        in_specs=in_specs,
        out_specs=out_specs,
        out_shape=out_shape,
        scratch_shapes=[pltpu.VMEM((tm, d), BF16)],
        compiler_params=pltpu.CompilerParams(
            dimension_semantics=("parallel", "arbitrary"), vmem_limit_bytes=VMEM_LIMIT),
        name="norm_matmul_small" if has_small else "norm_matmul",
    )(*args)


CUMSUM_BLOCK = 256
BIAS_TERMS = 3


def _bias_lane(head):
    return jnp.where(head % HEADS_PER_BLOCK == 0, HEAD_DIM, 0)


def _fox_bias_kernel(f_ref, b_ref, o_ref):
    t = f_ref.shape[1]
    row = lax.broadcasted_iota(jnp.int32, (CUMSUM_BLOCK, CUMSUM_BLOCK), 0)
    col = lax.broadcasted_iota(jnp.int32, (CUMSUM_BLOCK, CUMSUM_BLOCK), 1)
    tri = (row >= col).astype(BF16)
    packed_lane = lax.broadcasted_iota(jnp.int32, (LANES, HEAD_W), 0)
    lane = lax.broadcasted_iota(jnp.int32, (LANES, HEAD_W), 1)
    head = packed_lane % HEADS
    term = packed_lane // HEADS
    place = ((term < BIAS_TERMS) & (lane // LANES == head // HEADS_PER_BLOCK)
             & (lane % LANES == _bias_lane(head) + term)).astype(BF16)
    is_head = lax.broadcasted_iota(jnp.int32, (CUMSUM_BLOCK, LANES), 1) < HEADS
    carry = jnp.zeros((1, LANES), F32)
    for blk in range(t // CUMSUM_BLOCK):
        sl = slice(blk * CUMSUM_BLOCK, (blk + 1) * CUMSUM_BLOCK)
        log_f = jax.nn.log_sigmoid(f_ref[0, sl, :] + b_ref[...])
        c = _dot_exact_lhs(tri, log_f) + carry
        carry = c[CUMSUM_BLOCK - 1:CUMSUM_BLOCK, :]
        terms = _split(jnp.where(is_head, c * (-(HEAD_DIM ** 0.5)), 0.0), BIAS_TERMS)
        packed = terms[0].astype(F32)
        for j in range(1, BIAS_TERMS):
            packed = packed + pltpu.roll(terms[j].astype(F32), j * HEADS, 1)
        o_ref[0, sl, :] = _dot(packed.astype(BF16), place).astype(o_ref.dtype)


def _fox_bias(small, f_bias_padded, batch, seq):
    return pl.pallas_call(
        _fox_bias_kernel,
        grid=(batch,),
        in_specs=[
            pl.BlockSpec((1, seq, LANES), lambda b: (b, 0, 1)),
            pl.BlockSpec((1, LANES), lambda b: (0, 0)),
        ],
        out_specs=pl.BlockSpec((1, seq, HEAD_W), lambda b: (b, 0, 0)),
        out_shape=jax.ShapeDtypeStruct((batch, seq, HEAD_W), BF16),
        compiler_params=pltpu.CompilerParams(
            dimension_semantics=("parallel",), vmem_limit_bytes=VMEM_LIMIT),
        name="fox_bias",
    )(small, f_bias_padded)


MASK_VALUE = -1e30
LOG2E = 1.4426950408889634


VT_ROWS = HEAD_DIM + 16
FOX_BLOCKS = 2


def _fox_attn_kernel(q_ref, k_ref, v_ref, bias_ref, g_ref, o_ref,
                     kaug_scr, vt_scr, *, blk):
    qi = pl.program_id(2)
    seq = k_ref.shape[1]
    nh = FOX_BLOCKS * HEADS_PER_BLOCK

    @pl.when(qi == 0)
    def _():
        lane = lax.broadcasted_iota(jnp.int32, (seq, LANES), 1)
        ones_row = (lax.broadcasted_iota(jnp.int32, (VT_ROWS - HEAD_DIM, blk), 0) == 0).astype(BF16)
        for p in range(FOX_BLOCKS):
            ls = slice(p * LANES, (p + 1) * LANES)
            k = k_ref[0, :, ls]
            bias = bias_ref[0, :, ls]
            kaug_scr[HEADS_PER_BLOCK * p] = jnp.where(lane < HEAD_DIM, k, bias)
            kaug_scr[HEADS_PER_BLOCK * p + 1] = jnp.where(lane >= HEAD_DIM, k, bias)
            for j in range(seq // blk):
                v_t = v_ref[0, j * blk:(j + 1) * blk, ls].T
                for h in range(HEADS_PER_BLOCK):
                    vt_scr[j, HEADS_PER_BLOCK * p + h] = jnp.concatenate(
                        [v_t[h * HEAD_DIM:(h + 1) * HEAD_DIM], ones_row], axis=0)

    lane = lax.broadcasted_iota(jnp.int32, (blk, LANES), 1)
    ones0 = ((lane >= HEAD_DIM) & (lane < HEAD_DIM + BIAS_TERMS)).astype(BF16)
    ones1 = (lane < BIAS_TERMS).astype(BF16)
    q_aug = []
    for p in range(FOX_BLOCKS):
        q = q_ref[0, :, p * LANES:(p + 1) * LANES]
        q_aug += [jnp.where(lane < HEAD_DIM, q, ones0), jnp.where(lane >= HEAD_DIM, q, ones1)]
    q_aug = jnp.stack(q_aug)
    kappa = (HEAD_DIM ** -0.5) * LOG2E

    half = blk // 2

    def scores(step):
        (k0, k1), (q0, q1) = step
        return _bdot_nt(kaug_scr[:, k0:k1, :], q_aug[:, q0:q1, :])

    def softmax_step(s_t, m_prev, masked):
        if masked:
            key = lax.broadcasted_iota(jnp.int32, s_t.shape[1:], 0)
            qry = lax.broadcasted_iota(jnp.int32, s_t.shape[1:], 1)
            s_t = jnp.where(key <= qry, s_t, MASK_VALUE)
        m_new = jnp.maximum(m_prev, jnp.max(s_t, axis=1, keepdims=True))
        alpha = jnp.exp2((m_prev - m_new) * kappa)
        return jnp.exp2((s_t - m_new) * kappa).astype(BF16), alpha, m_new

    def attend(last):
        steps = [((j * blk, (j + 1) * blk), (0, blk)) for j in range(last)]
        steps += [((last * blk, last * blk + half), (0, blk)),
                  ((last * blk + half, (last + 1) * blk), (half, blk))]
        m = jnp.full((nh, 1, blk), MASK_VALUE, F32)
        acc = jnp.zeros((nh, VT_ROWS, blk), F32)

        def accumulate(acc, step, p_t, alpha):
            (k0, k1), (q0, q1) = step
            v_t = vt_scr[k0 // blk][:, :, k0 % blk:k0 % blk + (k1 - k0)]
            new = alpha * acc[:, :, q0:q1] + _bdot(v_t, p_t)
            return new if q0 == 0 else jnp.concatenate([acc[:, :, :q0], new], axis=2)

        s_next = scores(steps[0])
        pending = None
        for i, step in enumerate(steps):
            s_cur = s_next
            if i + 1 < len(steps):
                s_next = scores(steps[i + 1])
            q0, q1 = step[1]
            p_t, alpha, m_part = softmax_step(s_cur, m[:, :, q0:q1], masked=(i >= last))
            m = m_part if q0 == 0 else jnp.concatenate([m[:, :, :q0], m_part], axis=2)
            if pending is not None:
                acc = accumulate(acc, *pending)
            pending = (step, p_t, alpha)
        acc = accumulate(acc, *pending)
        o_t = acc[:, :HEAD_DIM, :] * (1.0 / acc[:, HEAD_DIM:HEAD_DIM + 1, :])
        o = o_t.reshape(nh * HEAD_DIM, blk).T
        o_ref[0] = (o * _silu(g_ref[0].astype(F32))).astype(o_ref.dtype)

    for last in range(seq // blk):
        pl.when(qi == last)(functools.partial(attend, last))


def _fox_attention(main, bias, batch, seq, *, blk=512):
    wide = FOX_BLOCKS * LANES
    nblk = HEAD_W // wide
    nq = seq // blk
    return pl.pallas_call(
        functools.partial(_fox_attn_kernel, blk=blk),
        grid=(batch, nblk, nq),
        in_specs=[
            pl.BlockSpec((1, blk, wide), lambda b, hp, qi: (b, qi, hp)),
            pl.BlockSpec((1, seq, wide), lambda b, hp, qi: (b, 0, nblk + hp)),
            pl.BlockSpec((1, seq, wide), lambda b, hp, qi: (b, 0, 2 * nblk + hp)),
            pl.BlockSpec((1, seq, wide), lambda b, hp, qi: (b, 0, hp)),
            pl.BlockSpec((1, blk, wide), lambda b, hp, qi: (b, qi, 3 * nblk + hp)),
        ],
        out_specs=pl.BlockSpec((1, blk, wide), lambda b, hp, qi: (b, qi, hp)),
        out_shape=jax.ShapeDtypeStruct((batch, seq, HEAD_W), BF16),
        scratch_shapes=[
            pltpu.VMEM((FOX_BLOCKS * HEADS_PER_BLOCK, seq, LANES), BF16),
            pltpu.VMEM((seq // blk, FOX_BLOCKS * HEADS_PER_BLOCK, VT_ROWS, blk), BF16),
        ],
        compiler_params=pltpu.CompilerParams(
            dimension_semantics=("parallel", "parallel", "arbitrary"),
            vmem_limit_bytes=VMEM_LIMIT),
        name="fox_attention",
    )(main, main, main, bias, main)


def _head_sum(x, terms):
    width = x.shape[-1]
    row = lax.broadcasted_iota(jnp.int32, (width, width), 0)
    col = lax.broadcasted_iota(jnp.int32, (width, width), 1)
    same_head = (row // HEAD_DIM == col // HEAD_DIM).astype(BF16)
    return sum(_dot(part, same_head) for part in _split(x, terms))


def _stack_heads(x):
    head = lax.broadcasted_iota(jnp.int32, x.shape, x.ndim - 1) // HEAD_DIM
    zero = jnp.zeros_like(x)
    return jnp.concatenate(
        [jnp.where(head == h, x, zero) for h in range(x.shape[-1] // HEAD_DIM)], axis=x.ndim - 2)


def _bdot(a, b):
    return lax.dot_general(a, b, (((2,), (1,)), ((0,), (0,))), preferred_element_type=F32)


def _bdot_nt(a, b):
    return lax.dot_general(a, b, (((2,), (2,)), ((0,), (0,))), preferred_element_type=F32)


def _bdot_tn(a, b):
    return lax.dot_general(a, b, (((1,), (1,)), ((0,), (0,))), preferred_element_type=F32)


def _unit_lower_inverse(l_strict, order):
    c, width = l_strict.shape[1:]
    eye = (lax.broadcasted_iota(jnp.int32, (c, width), 0)
           == lax.broadcasted_iota(jnp.int32, (c, width), 1) % HEAD_DIM).astype(F32)
    inv = eye + l_strict
    power = l_strict.astype(BF16)
    power = _bdot(power, _stack_heads(power)).astype(BF16)
    yield
    last = order.bit_length() - 2
    for k in range(1, last):
        both = _bdot(jnp.concatenate([inv.astype(BF16), power], axis=1), _stack_heads(power))
        inv = inv + both[:, :c]
        power = both[:, c:].astype(BF16)
        yield
    return inv + _bdot(inv.astype(BF16), _stack_heads(power))


RWKV_WIDTH = 128
RWKV_BLOCKS = 8
RWKV_ROWS = 128
RWKV_STREAM_SKEW = 14


def _interleave(streams, skew):
    waiting = list(streams)
    live = []
    age = 0
    while waiting or live:
        if waiting and age % skew == 0:
            live.append(waiting.pop(0))
        age += 1
        for gen in list(live):
            try:
                next(gen)
            except StopIteration:
                live.remove(gen)


def _rwkv_kernel(r_ref, k_ref, v_ref, g_ref, wa_ref, mu_ref, par_ref, w2_ref, a2_ref,
                 o_ref, state_scr, prev_scr, *, tc):
    ti = pl.program_id(2)
    nb = RWKV_BLOCKS
    wd = RWKV_WIDTH
    c = RWKV_CHUNK
    rows = RWKV_ROWS
    nc = rows // c

    @pl.when(ti == 0)
    def _():
        state_scr[...] = jnp.zeros(state_scr.shape, F32)
        prev_scr[...] = jnp.zeros(prev_scr.shape, F32)

    def shift_lerp(x, prev_row, mu):
        rolled = pltpu.roll(x, 1, x.ndim - 2)
        first = lax.broadcasted_iota(jnp.int32, x.shape, x.ndim - 2) == 0
        return x + (jnp.where(first, prev_row, rolled) - x) * mu

    def head_sum(x, terms):
        return _head_sum(x.reshape(nb * rows, wd), terms).reshape(nb, rows, wd)

    def per_block(x):
        return jnp.stack([x[:, i * wd:(i + 1) * wd] for i in range(nb)])

    def chunked(x):
        return x.reshape(nb * nc, c, wd)

    def of_chunk(x, ci):
        return x.reshape((nb, nc) + x.shape[1:])[:, ci]

    w0, a0, k_k, k_a, r_k, ln_g, ln_b = (par_ref[:, i:i + 1, :] for i in range(7))
    row = lax.broadcasted_iota(jnp.int32, (rows, rows), 0)
    col = lax.broadcasted_iota(jnp.int32, (rows, rows), 1)
    tri_blk = ((row >= col) & (row // c == col // c)).astype(BF16)
    trow = lax.broadcasted_iota(jnp.int32, (c, wd), 0)
    tcol = lax.broadcasted_iota(jnp.int32, (c, wd), 1) % HEAD_DIM
    lower_incl = trow >= tcol
    lower_strict = trow > tcol

    def stream(si):
        rs = slice(si * rows, (si + 1) * rows)

        def blocks(ref, dtype=None):
            x = jnp.stack([ref[0, rs, i * wd:(i + 1) * wd] for i in range(nb)])
            return x if dtype is None else x.astype(dtype)

        r_in = blocks(r_ref, F32)
        r = shift_lerp(r_in, prev_scr[0], mu_ref[:, 0:1, :])
        prev_scr[0] = r_in[:, rows - 1:rows, :]
        yield
        k_in = blocks(k_ref, F32)
        k = shift_lerp(k_in, prev_scr[1], mu_ref[:, 1:2, :])
        prev_scr[1] = k_in[:, rows - 1:rows, :]
        yield
        v_in = blocks(v_ref, F32)
        v = shift_lerp(v_in, prev_scr[2], mu_ref[:, 2:3, :])
        prev_scr[2] = v_in[:, rows - 1:rows, :]
        wa_in = wa_ref[0, rs, :]
        wa = shift_lerp(wa_in, prev_scr[3, 0, :, :LANES], mu_ref[0, 3:4, :LANES])
        prev_scr[3, 0, :, :LANES] = wa_in[rows - 1:rows, :]
        yield
        w_lo = jnp.tanh(wa[:, :LORA]).astype(BF16)
        a_lo = wa[:, LORA:].astype(BF16)
        w_raw = -jax.nn.softplus(-(w0 + per_block(_dot(w_lo, w2_ref[...])))) - 0.5
        logw = -jnp.exp(w_raw)
        yield
        a = jax.nn.sigmoid(a0 + per_block(_dot(a_lo, a2_ref[...])))
        kk = k * k_k
        kk = kk * lax.rsqrt(jnp.maximum(head_sum(kk * kk, 1), 1e-24))
        yield
        k = k * (1.0 + (a - 1.0) * k_a)
        bonus = head_sum(r * k * r_k, 1) * v
        yield
        cum = jnp.stack([_dot_exact_lhs(tri_blk, logw[b], 2) for b in range(nb)])
        yield
        cum3 = chunked(cum)
        cum_end = cum3[:, c - 1:c, :]
        e_neg = jnp.exp(-cum3)
        e_end = jnp.exp(cum_end - cum3)
        akk = chunked(a * kk)
        k3 = chunked(k)
        r_t = chunked(r) * jnp.exp(cum3)
        yield
        p_t = (-chunked(kk) * jnp.exp(cum3 - chunked(logw))).astype(BF16)
        p_s = _stack_heads(p_t)
        z_s = _stack_heads((akk * e_neg).astype(BF16))
        yield
        k_s = _stack_heads((k3 * e_neg).astype(BF16))
        zbar_s = _stack_heads((akk * e_end).astype(BF16))
        kbar_s = _stack_heads((k3 * e_end).astype(BF16))
        v_s = _stack_heads(chunked(v).astype(BF16))
        g_end = jnp.exp(cum_end)
        yield
        pr = jnp.concatenate([p_t, r_t.astype(BF16)], axis=1)
        zk = jnp.concatenate([z_s, k_s], axis=1)
        amat = _bdot_nt(pr, zk)
        l_pz = jnp.where(lower_strict, amat[:, :c, :wd], 0.0)
        l_pk = jnp.where(lower_strict, amat[:, :c, wd:], 0.0).astype(BF16)
        l_rz = jnp.where(lower_incl, amat[:, c:, :wd], 0.0).astype(BF16)
        l_rk = jnp.where(lower_incl, amat[:, c:, wd:], 0.0).astype(BF16)
        yield
        inv = (yield from _unit_lower_inverse(l_pz, c)).astype(BF16)
        lv = _bdot(l_pk, v_s).astype(BF16)
        yield
        tp_ui = _bdot(inv, jnp.concatenate([p_s, _stack_heads(lv)], axis=2)).astype(BF16)
        tp_ui_s = jnp.concatenate(
            [_stack_heads(tp_ui[:, :, :wd]), _stack_heads(tp_ui[:, :, wd:])], axis=2)
        yield
        rz = _bdot(l_rz, tp_ui_s)
        qeff = (r_t + rz[:, :, :wd]).astype(BF16)
        yi = rz[:, :, wd:] + _bdot(l_rk, v_s)
        yield
        mz = _bdot_tn(tp_ui_s, zbar_s)
        m_low = mz[:, :wd].astype(BF16)
        b_mat = mz[:, wd:] + _bdot_tn(v_s, kbar_s)
        yield
        s = state_scr[...]
        ys = []
        for ci in range(nc):
            s_b = s.astype(BF16)
            ys.append(_bdot_nt(of_chunk(qeff, ci), s_b) + of_chunk(yi, ci))
            s = s * of_chunk(g_end, ci) + _bdot(s_b, of_chunk(m_low, ci)) + of_chunk(b_mat, ci)
            yield
        state_scr[...] = s
        y = jnp.concatenate(ys, axis=1)
        mean = head_sum(y, 2) * (1.0 / HEAD_DIM)
        yc = y - mean
        yield
        var = head_sum(yc * yc, 1) * (1.0 / HEAD_DIM)
        y = yc * lax.rsqrt(var + RWKV_GN_EPS) * ln_g + ln_b + bonus
        yield
        y = y * _silu(blocks(g_ref, F32))
        for b in range(nb):
            o_ref[0, rs, b * wd:(b + 1) * wd] = y[b].astype(o_ref.dtype)

    _interleave([stream(si) for si in range(tc // rows)], RWKV_STREAM_SKEW)


def _rwkv(main, small, mu_rows, par_rows, w2, a2, batch, seq, *, tc=1024):
    nb = RWKV_BLOCKS
    wide = nb * RWKV_WIDTH
    ngrp = HEAD_W // wide
    return pl.pallas_call(
        functools.partial(_rwkv_kernel, tc=tc),
        grid=(batch, ngrp, seq // tc),
        in_specs=[
            pl.BlockSpec((1, tc, wide), lambda b, hg, ti: (b, ti, 4 * ngrp + hg)),
            pl.BlockSpec((1, tc, wide), lambda b, hg, ti: (b, ti, 5 * ngrp + hg)),
            pl.BlockSpec((1, tc, wide), lambda b, hg, ti: (b, ti, 6 * ngrp + hg)),
            pl.BlockSpec((1, tc, wide), lambda b, hg, ti: (b, ti, 7 * ngrp + hg)),
            pl.BlockSpec((1, tc, LANES), lambda b, hg, ti: (b, ti, 0)),
            pl.BlockSpec((nb, 8, RWKV_WIDTH), lambda b, hg, ti: (hg, 0, 0)),
            pl.BlockSpec((nb, 8, RWKV_WIDTH), lambda b, hg, ti: (hg, 0, 0)),
            pl.BlockSpec((LORA, wide), lambda b, hg, ti: (0, hg)),
            pl.BlockSpec((LORA, wide), lambda b, hg, ti: (0, hg)),
        ],
        out_specs=pl.BlockSpec((1, tc, wide), lambda b, hg, ti: (b, ti, hg)),
        out_shape=jax.ShapeDtypeStruct((batch, seq, HEAD_W), BF16),
        scratch_shapes=[
            pltpu.VMEM((nb, RWKV_WIDTH, RWKV_WIDTH), F32),
            pltpu.VMEM((4, nb, 1, RWKV_WIDTH), F32),
        ],
        compiler_params=pltpu.CompilerParams(
            dimension_semantics=("parallel", "parallel", "arbitrary"),
            vmem_limit_bytes=VMEM_LIMIT),
        name="rwkv7",
    )(main, main, main, main, small, mu_rows, par_rows, w2, a2)


def _sgu_kernel(u_ref, v_ref, g_ref, lng_ref, lnb_ref, ws_ref, bs_ref, o_ref, *, rows):
    c = GMLP_CHUNK
    causal = (lax.broadcasted_iota(jnp.int32, (c, c), 0)
              >= lax.broadcasted_iota(jnp.int32, (c, c), 1))
    v = _gelu(v_ref[...].astype(F32))
    mu = jnp.mean(v, axis=-1, keepdims=True)
    vc = v - mu
    var = jnp.mean(vc * vc, axis=-1, keepdims=True)
    vn = (vc * lax.rsqrt(var + LN_EPS) * lng_ref[...] + lnb_ref[...]).astype(BF16)
    for grp in range(GMLP_GROUPS):
        cs = slice(grp * c, (grp + 1) * c)
        w = jnp.where(causal, ws_ref[grp], 0.0).astype(BF16)
        bias = bs_ref[:, grp:grp + 1]
        for ch in range(rows // c):
            rs = slice(ch * c, (ch + 1) * c)
            mixed = _dot(w, vn[rs, cs]) + bias
            u = _gelu(u_ref[rs, cs].astype(F32))
            o_ref[rs, cs] = (u * mixed * _silu(g_ref[rs, cs].astype(F32))).astype(o_ref.dtype)


def _spatial_gating(proj, ln_g, ln_b, w_s, b_s_t, *, rows=512):
    m = proj.shape[0]
    d = D_MODEL
    return pl.pallas_call(
        functools.partial(_sgu_kernel, rows=rows),
        grid=(m // rows,),
        in_specs=[
            pl.BlockSpec((rows, d), lambda i: (i, 0)),
            pl.BlockSpec((rows, d), lambda i: (i, 1)),
            pl.BlockSpec((rows, d), lambda i: (i, 2)),
            pl.BlockSpec((1, d), lambda i: (0, 0)),
            pl.BlockSpec((1, d), lambda i: (0, 0)),
            pl.BlockSpec((GMLP_GROUPS, GMLP_CHUNK, GMLP_CHUNK), lambda i: (0, 0, 0)),
            pl.BlockSpec((GMLP_CHUNK, GMLP_GROUPS), lambda i: (0, 0)),
        ],
        out_specs=pl.BlockSpec((rows, d), lambda i: (i, 0)),
        out_shape=jax.ShapeDtypeStruct((m, d), BF16),
        compiler_params=pltpu.CompilerParams(
            dimension_semantics=("parallel",), vmem_limit_bytes=VMEM_LIMIT),
        name="spatial_gating",
    )(proj, proj, proj, ln_g, ln_b, w_s, b_s_t)


def _tail_kernel(a0_ref, a1_ref, h_ref, p_ref, w0_ref, w1_ref, gp_ref, wp_ref, wg_ref, o_ref):
    y = _dot(a0_ref[...], w0_ref[...]) + _dot(a1_ref[...], w1_ref[...])
    ms = jnp.mean(y * y, axis=-1, keepdims=True)
    h1 = h_ref[...] + y * lax.rsqrt(ms + RMS_EPS) * gp_ref[...]
    gate = _dot(h1.astype(BF16), wg_ref[...])
    pp = _dot(p_ref[...].astype(BF16), wp_ref[...])
    o_ref[...] = h1 + pp * jax.nn.sigmoid(gate)


def _tail(act0, blk0, act1, blk1, h, p_all, layer, w_out, g_post, w_proj, w_gate, *, tm=512):
    m, d = h.shape
    p_row0 = layer * (m // tm)
    half = w_out.shape[0] // 2
    return pl.pallas_call(
        _tail_kernel,
        grid=(m // tm,),
        in_specs=[
            pl.BlockSpec((tm, half), lambda i: (i, blk0)),
            pl.BlockSpec((tm, half), lambda i: (i, blk1)),
            pl.BlockSpec((tm, d), lambda i: (i, 0)),
            pl.BlockSpec((tm, PLE_DIM), lambda i: (p_row0 + i, 0)),
            _resident((half, d), lambda i: (0, 0)),
            _resident((half, d), lambda i: (1, 0)),
            _resident((1, d), lambda i: (0, 0)),
            _resident((PLE_DIM, d), lambda i: (0, 0)),
            _resident((d, d), lambda i: (0, 0)),
        ],
        out_specs=pl.BlockSpec((tm, d), lambda i: (i, 0)),
        out_shape=jax.ShapeDtypeStruct((m, d), F32),
        compiler_params=pltpu.CompilerParams(
            dimension_semantics=("parallel",), vmem_limit_bytes=VMEM_LIMIT),
        name="tail",
    )(act0, act1, h, p_all, w_out, w_out, g_post, w_proj, w_gate)


def _per_block_rows(vecs):
    nblk = HEAD_W // RWKV_WIDTH
    rows = [v.reshape(nblk, 1, RWKV_WIDTH) for v in vecs]
    rows += [jnp.zeros((nblk, 1, RWKV_WIDTH), F32)] * (8 - len(rows))
    return jnp.concatenate(rows, axis=1)


def kernel(x, p, norm_pre, norm_post, ab_w_in, fox_f_bias, rwkv_mu, rwkv_w0, rwkv_w2, rwkv_a0, rwkv_a2, rwkv_k_k, rwkv_k_a, rwkv_r_k, rwkv_ln_g, rwkv_ln_b, ab_w_out, c_w_in, c_ln_g, c_ln_b, c_w_s, c_b_s, c_w_out, ple_w_proj, ple_w_gate):
    batch, seq, d = x.shape
    m = batch * seq
    w = HEAD_W
    h = x.reshape(m, d)

    w_in = ab_w_in[0]
    o_f = 3 * w
    o_ga = o_f + HEADS
    o_sh = o_ga + w
    o_lo = o_sh + 3 * w
    o_gb = o_lo + 2 * LORA
    w_bf = w_in.astype(BF16)
    w_main = jnp.concatenate([w_bf[:, :o_f], w_bf[:, o_ga:o_lo], w_bf[:, o_gb:]], axis=1)
    w_small = jnp.concatenate(
        [w_bf[:, o_lo:o_gb], w_bf[:, o_f:o_ga], jnp.zeros((d, LANES - HEADS), BF16)], axis=1)
    main, small = _norm_matmul(h, norm_pre[0:1], w_main, w_small)
    main3 = main.reshape(batch, seq, 8 * w)
    small3 = small.reshape(batch, seq, 2 * LANES)

    f_bias = jnp.concatenate([fox_f_bias[0], jnp.zeros((LANES - HEADS,), F32)]).reshape(1, LANES)
    fox_bias = _fox_bias(small3, f_bias, batch, seq)
    act_a = _fox_attention(main3, fox_bias, batch, seq)

    mu = rwkv_mu[0]
    mu_rows = _per_block_rows([mu[:w], mu[w:2 * w], mu[2 * w:3 * w]])
    mu_rows = mu_rows.at[:, 3, :2 * LORA].set(mu[3 * w:])
    par_rows = _per_block_rows([rwkv_w0[0], rwkv_a0[0], rwkv_k_k[0], rwkv_k_a[0],
                                rwkv_r_k[0].reshape(w), rwkv_ln_g[0], rwkv_ln_b[0]])
    act_b = _rwkv(main3, small3, mu_rows, par_rows, rwkv_w2[0].astype(BF16),
                  rwkv_a2[0].astype(BF16), batch, seq)

    p_all = p.reshape(p.shape[0] * m, PLE_DIM)
    h = _tail(act_a.reshape(m, w), 0, act_b.reshape(m, w), 0, h, p_all, 0,
              ab_w_out[0].astype(BF16), norm_post[0:1], ple_w_proj[0].astype(BF16),
              ple_w_gate[0].astype(BF16))

    proj = _norm_matmul(h, norm_pre[1:2], c_w_in[0].astype(BF16))[0]
    act_c = _spatial_gating(proj, c_ln_g[0:1], c_ln_b[0:1], c_w_s[0], c_b_s[0].T)
    h = _tail(act_c, 0, act_c, 1, h, p_all, 1,
              c_w_out[0].astype(BF16), norm_post[1:2], ple_w_proj[1].astype(BF16),
              ple_w_gate[1].astype(BF16))
    return h.reshape(batch, seq, d)
```

```python
import functools

import jax
import jax.numpy as jnp
from jax import lax
from jax.experimental import pallas as pl
from jax.experimental.pallas import tpu as pltpu

F32 = jnp.float32
BF16 = jnp.bfloat16

D_MODEL = 2048
PLE_DIM = 256
HEADS = 16
HEAD_DIM = 64
HEAD_W = HEADS * HEAD_DIM
LORA = 64
GMLP_GROUPS = 16
GMLP_CHUNK = 128
RMS_EPS = 1e-6
LN_EPS = 1e-5
RWKV_GN_EPS = 64e-5
LANES = 128
HEADS_PER_BLOCK = LANES // HEAD_DIM
RWKV_CHUNK = 64
VMEM_LIMIT = 56 * 1024 * 1024


def _dot(a, b):
    return jnp.dot(a, b, preferred_element_type=F32)


def _split(x, terms):
    parts = []
    for _ in range(terms - 1):
        part = x.astype(BF16)
        parts.append(part)
        x = x - part.astype(F32)
    parts.append(x.astype(BF16))
    return parts


def _dot_exact_lhs(a_bf16, x, terms=3):
    return sum(_dot(a_bf16, part) for part in _split(x, terms))


def _resident(block_shape, index_map):
    return pl.BlockSpec(block_shape, index_map, pipeline_mode=pl.Buffered(1))


def _silu(x):
    return x * jax.nn.sigmoid(x)


def _gelu(x):
    return 0.5 * x * (1.0 + lax.erf(x * (0.5 ** 0.5)))


def _norm_mm_kernel(x_ref, g_ref, w_ref, *rest, has_small):
    if has_small:
        ws_ref, o_ref, os_ref, xn_ref = rest
    else:
        o_ref, xn_ref = rest

    @pl.when(pl.program_id(1) == 0)
    def _():
        x = x_ref[...]
        ms = jnp.mean(x * x, axis=-1, keepdims=True)
        xn = (x * lax.rsqrt(ms + RMS_EPS) * g_ref[...]).astype(BF16)
        xn_ref[...] = xn
        if has_small:
            os_ref[...] = _dot(xn, ws_ref[...])

    o_ref[...] = _dot(xn_ref[...], w_ref[...]).astype(o_ref.dtype)


def _norm_matmul(x, g, w, w_small=None, *, tm=1024, tn=2048):
    m, d = x.shape
    n = w.shape[1]
    has_small = w_small is not None
    in_specs = [
        pl.BlockSpec((tm, d), lambda i, j: (i, 0)),
        _resident((1, d), lambda i, j: (0, 0)),
        pl.BlockSpec((d, tn), lambda i, j: (0, j)),
    ]
    out_specs = [pl.BlockSpec((tm, tn), lambda i, j: (i, j))]
    out_shape = [jax.ShapeDtypeStruct((m, n), BF16)]
    args = [x, g, w]
    if has_small:
        ns = w_small.shape[1]
        in_specs.append(_resident((d, ns), lambda i, j: (0, 0)))
        out_specs.append(pl.BlockSpec((tm, ns), lambda i, j: (i, 0)))
        out_shape.append(jax.ShapeDtypeStruct((m, ns), F32))
        args.append(w_small)
    return pl.pallas_call(
        functools.partial(_norm_mm_kernel, has_small=has_small),
        grid=(m // tm, n // tn),
        in_specs=in_specs,
        out_specs=out_specs,
        out_shape=out_shape,
        scratch_shapes=[pltpu.VMEM((tm, d), BF16)],
        compiler_params=pltpu.CompilerParams(
            dimension_semantics=("parallel", "arbitrary"), vmem_limit_bytes=VMEM_LIMIT),
        name="norm_matmul_small" if has_small else "norm_matmul",
    )(*args)


CUMSUM_BLOCK = 256
BIAS_TERMS = 3


def _bias_lane(head):
    return jnp.where(head % HEADS_PER_BLOCK == 0, HEAD_DIM, 0)


def _fox_bias_kernel(f_ref, b_ref, o_ref):
    t = f_ref.shape[1]
    row = lax.broadcasted_iota(jnp.int32, (CUMSUM_BLOCK, CUMSUM_BLOCK), 0)
    col = lax.broadcasted_iota(jnp.int32, (CUMSUM_BLOCK, CUMSUM_BLOCK), 1)
    tri = (row >= col).astype(BF16)
    packed_lane = lax.broadcasted_iota(jnp.int32, (LANES, HEAD_W), 0)
    lane = lax.broadcasted_iota(jnp.int32, (LANES, HEAD_W), 1)
    head = packed_lane % HEADS
    term = packed_lane // HEADS
    place = ((term < BIAS_TERMS) & (lane // LANES == head // HEADS_PER_BLOCK)
             & (lane % LANES == _bias_lane(head) + term)).astype(BF16)
    is_head = lax.broadcasted_iota(jnp.int32, (CUMSUM_BLOCK, LANES), 1) < HEADS
    carry = jnp.zeros((1, LANES), F32)
    for blk in range(t // CUMSUM_BLOCK):
        sl = slice(blk * CUMSUM_BLOCK, (blk + 1) * CUMSUM_BLOCK)
        log_f = jax.nn.log_sigmoid(f_ref[0, sl, :] + b_ref[...])
        c = _dot_exact_lhs(tri, log_f) + carry
        carry = c[CUMSUM_BLOCK - 1:CUMSUM_BLOCK, :]
        terms = _split(jnp.where(is_head, c * (-(HEAD_DIM ** 0.5)), 0.0), BIAS_TERMS)
        packed = terms[0].astype(F32)
        for j in range(1, BIAS_TERMS):
            packed = packed + pltpu.roll(terms[j].astype(F32), j * HEADS, 1)
        o_ref[0, sl, :] = _dot(packed.astype(BF16), place).astype(o_ref.dtype)


def _fox_bias(small, f_bias_padded, batch, seq):
    return pl.pallas_call(
        _fox_bias_kernel,
        grid=(batch,),
        in_specs=[
            pl.BlockSpec((1, seq, LANES), lambda b: (b, 0, 1)),
            pl.BlockSpec((1, LANES), lambda b: (0, 0)),
        ],
        out_specs=pl.BlockSpec((1, seq, HEAD_W), lambda b: (b, 0, 0)),
        out_shape=jax.ShapeDtypeStruct((batch, seq, HEAD_W), BF16),
        compiler_params=pltpu.CompilerParams(
            dimension_semantics=("parallel",), vmem_limit_bytes=VMEM_LIMIT),
        name="fox_bias",
    )(small, f_bias_padded)


MASK_VALUE = -1e30
LOG2E = 1.4426950408889634


VT_ROWS = HEAD_DIM + 16
FOX_BLOCKS = 4


def _fox_attn_kernel(q_ref, k_ref, v_ref, bias_ref, g_ref, o_ref,
                     kaug_scr, vt_scr, *, blk):
    qi = pl.program_id(2)
    seq = k_ref.shape[1]
    nh = FOX_BLOCKS * HEADS_PER_BLOCK

    @pl.when(qi == 0)
    def _():
        lane = lax.broadcasted_iota(jnp.int32, (seq, LANES), 1)
        ones_row = (lax.broadcasted_iota(jnp.int32, (VT_ROWS - HEAD_DIM, blk), 0) == 0).astype(BF16)
        for p in range(FOX_BLOCKS):
            ls = slice(p * LANES, (p + 1) * LANES)
            k = k_ref[0, :, ls]
            bias = bias_ref[0, :, ls]
            kaug_scr[HEADS_PER_BLOCK * p] = jnp.where(lane < HEAD_DIM, k, bias)
            kaug_scr[HEADS_PER_BLOCK * p + 1] = jnp.where(lane >= HEAD_DIM, k, bias)
            for j in range(seq // blk):
                v_t = v_ref[0, j * blk:(j + 1) * blk, ls].T
                for h in range(HEADS_PER_BLOCK):
                    vt_scr[j, HEADS_PER_BLOCK * p + h] = jnp.concatenate(
                        [v_t[h * HEAD_DIM:(h + 1) * HEAD_DIM], ones_row], axis=0)

    lane = lax.broadcasted_iota(jnp.int32, (blk, LANES), 1)
    ones0 = ((lane >= HEAD_DIM) & (lane < HEAD_DIM + BIAS_TERMS)).astype(BF16)
    ones1 = (lane < BIAS_TERMS).astype(BF16)
    q_aug = []
    for p in range(FOX_BLOCKS):
        q = q_ref[0, :, p * LANES:(p + 1) * LANES]
        q_aug += [jnp.where(lane < HEAD_DIM, q, ones0), jnp.where(lane >= HEAD_DIM, q, ones1)]
    q_aug = jnp.stack(q_aug)
    kappa = (HEAD_DIM ** -0.5) * LOG2E

    half = blk // 2

    def scores(step):
        (k0, k1), (q0, q1) = step
        return _bdot_nt(kaug_scr[:, k0:k1, :], q_aug[:, q0:q1, :])

    def softmax_step(s_t, m_prev, masked):
        if masked:
            key = lax.broadcasted_iota(jnp.int32, s_t.shape[1:], 0)
            qry = lax.broadcasted_iota(jnp.int32, s_t.shape[1:], 1)
            s_t = jnp.where(key <= qry, s_t, MASK_VALUE)
        m_new = jnp.maximum(m_prev, jnp.max(s_t, axis=1, keepdims=True))
        alpha = jnp.exp2((m_prev - m_new) * kappa)
        return jnp.exp2((s_t - m_new) * kappa).astype(BF16), alpha, m_new

    def attend(last):
        steps = [((j * blk, (j + 1) * blk), (0, blk)) for j in range(last)]
        steps += [((last * blk, last * blk + half), (0, blk)),
                  ((last * blk + half, (last + 1) * blk), (half, blk))]
        m = jnp.full((nh, 1, blk), MASK_VALUE, F32)
        acc = jnp.zeros((nh, VT_ROWS, blk), F32)

        def accumulate(acc, step, p_t, alpha):
            (k0, k1), (q0, q1) = step
            v_t = vt_scr[k0 // blk][:, :, k0 % blk:k0 % blk + (k1 - k0)]
            new = alpha * acc[:, :, q0:q1] + _bdot(v_t, p_t)
            return new if q0 == 0 else jnp.concatenate([acc[:, :, :q0], new], axis=2)

        s_next = scores(steps[0])
        pending = None
        for i, step in enumerate(steps):
            s_cur = s_next
            if i + 1 < len(steps):
                s_next = scores(steps[i + 1])
            q0, q1 = step[1]
            p_t, alpha, m_part = softmax_step(s_cur, m[:, :, q0:q1], masked=(i >= last))
            m = m_part if q0 == 0 else jnp.concatenate([m[:, :, :q0], m_part], axis=2)
            if pending is not None:
                acc = accumulate(acc, *pending)
            pending = (step, p_t, alpha)
        acc = accumulate(acc, *pending)
        o_t = acc[:, :HEAD_DIM, :] * (1.0 / acc[:, HEAD_DIM:HEAD_DIM + 1, :])
        o = o_t.reshape(nh * HEAD_DIM, blk).T
        o_ref[0] = (o * _silu(g_ref[0].astype(F32))).astype(o_ref.dtype)

    for last in range(seq // blk):
        pl.when(qi == last)(functools.partial(attend, last))


def _fox_attention(main, bias, batch, seq, *, blk=512):
    wide = FOX_BLOCKS * LANES
    nblk = HEAD_W // wide
    nq = seq // blk
    return pl.pallas_call(
        functools.partial(_fox_attn_kernel, blk=blk),
        grid=(batch, nblk, nq),
        in_specs=[
            pl.BlockSpec((1, blk, wide), lambda b, hp, qi: (b, qi, hp)),
            pl.BlockSpec((1, seq, wide), lambda b, hp, qi: (b, 0, nblk + hp)),
            pl.BlockSpec((1, seq, wide), lambda b, hp, qi: (b, 0, 2 * nblk + hp)),
            pl.BlockSpec((1, seq, wide), lambda b, hp, qi: (b, 0, hp)),
            pl.BlockSpec((1, blk, wide), lambda b, hp, qi: (b, qi, 3 * nblk + hp)),
        ],
        out_specs=pl.BlockSpec((1, blk, wide), lambda b, hp, qi: (b, qi, hp)),
        out_shape=jax.ShapeDtypeStruct((batch, seq, HEAD_W), BF16),
        scratch_shapes=[
            pltpu.VMEM((FOX_BLOCKS * HEADS_PER_BLOCK, seq, LANES), BF16),
            pltpu.VMEM((seq // blk, FOX_BLOCKS * HEADS_PER_BLOCK, VT_ROWS, blk), BF16),
        ],
        compiler_params=pltpu.CompilerParams(
            dimension_semantics=("parallel", "parallel", "arbitrary"),
            vmem_limit_bytes=VMEM_LIMIT),
        name="fox_attention",
    )(main, main, main, bias, main)


def _head_sum(x, terms):
    width = x.shape[-1]
    row = lax.broadcasted_iota(jnp.int32, (width, width), 0)
    col = lax.broadcasted_iota(jnp.int32, (width, width), 1)
    same_head = (row // HEAD_DIM == col // HEAD_DIM).astype(BF16)
    return sum(_dot(part, same_head) for part in _split(x, terms))


def _stack_heads(x):
    head = lax.broadcasted_iota(jnp.int32, x.shape, x.ndim - 1) // HEAD_DIM
    zero = jnp.zeros_like(x)
    return jnp.concatenate(
        [jnp.where(head == h, x, zero) for h in range(x.shape[-1] // HEAD_DIM)], axis=x.ndim - 2)


def _bdot(a, b):
    return lax.dot_general(a, b, (((2,), (1,)), ((0,), (0,))), preferred_element_type=F32)


def _bdot_nt(a, b):
    return lax.dot_general(a, b, (((2,), (2,)), ((0,), (0,))), preferred_element_type=F32)


def _bdot_tn(a, b):
    return lax.dot_general(a, b, (((1,), (1,)), ((0,), (0,))), preferred_element_type=F32)


def _unit_lower_inverse(l_strict, order):
    c, width = l_strict.shape[1:]
    eye = (lax.broadcasted_iota(jnp.int32, (c, width), 0)
           == lax.broadcasted_iota(jnp.int32, (c, width), 1) % HEAD_DIM).astype(F32)
    inv = eye + l_strict
    power = l_strict.astype(BF16)
    power = _bdot(power, _stack_heads(power)).astype(BF16)
    yield
    last = order.bit_length() - 2
    for k in range(1, last):
        both = _bdot(jnp.concatenate([inv.astype(BF16), power], axis=1), _stack_heads(power))
        inv = inv + both[:, :c]
        power = both[:, c:].astype(BF16)
        yield
    return inv + _bdot(inv.astype(BF16), _stack_heads(power))


RWKV_WIDTH = 128
RWKV_BLOCKS = 8
RWKV_ROWS = 128
RWKV_STREAM_SKEW = 14


def _interleave(streams, skew):
    waiting = list(streams)
    live = []
    age = 0
    while waiting or live:
        if waiting and age % skew == 0:
            live.append(waiting.pop(0))
        age += 1
        for gen in list(live):
            try:
                next(gen)
            except StopIteration:
                live.remove(gen)


def _rwkv_kernel(r_ref, k_ref, v_ref, g_ref, wa_ref, mu_ref, par_ref, w2_ref, a2_ref,
                 o_ref, state_scr, prev_scr, *, tc):
    ti = pl.program_id(2)
    nb = RWKV_BLOCKS
    wd = RWKV_WIDTH
    c = RWKV_CHUNK
    rows = RWKV_ROWS
    nc = rows // c

    @pl.when(ti == 0)
    def _():
        state_scr[...] = jnp.zeros(state_scr.shape, F32)
        prev_scr[...] = jnp.zeros(prev_scr.shape, F32)

    def shift_lerp(x, prev_row, mu):
        rolled = pltpu.roll(x, 1, x.ndim - 2)
        first = lax.broadcasted_iota(jnp.int32, x.shape, x.ndim - 2) == 0
        return x + (jnp.where(first, prev_row, rolled) - x) * mu

    def head_sum(x, terms):
        return _head_sum(x.reshape(nb * rows, wd), terms).reshape(nb, rows, wd)

    def per_block(x):
        return jnp.stack([x[:, i * wd:(i + 1) * wd] for i in range(nb)])

    def chunked(x):
        return x.reshape(nb * nc, c, wd)

    def of_chunk(x, ci):
        return x.reshape((nb, nc) + x.shape[1:])[:, ci]

    w0, a0, k_k, k_a, r_k, ln_g, ln_b = (par_ref[:, i:i + 1, :] for i in range(7))
    row = lax.broadcasted_iota(jnp.int32, (rows, rows), 0)
    col = lax.broadcasted_iota(jnp.int32, (rows, rows), 1)
    tri_blk = ((row >= col) & (row // c == col // c)).astype(BF16)
    trow = lax.broadcasted_iota(jnp.int32, (c, wd), 0)
    tcol = lax.broadcasted_iota(jnp.int32, (c, wd), 1) % HEAD_DIM
    lower_incl = trow >= tcol
    lower_strict = trow > tcol

    def stream(si):
        rs = slice(si * rows, (si + 1) * rows)

        def blocks(ref, dtype=None):
            x = jnp.stack([ref[0, rs, i * wd:(i + 1) * wd] for i in range(nb)])
            return x if dtype is None else x.astype(dtype)

        r_in = blocks(r_ref, F32)
        r = shift_lerp(r_in, prev_scr[0], mu_ref[:, 0:1, :])
        prev_scr[0] = r_in[:, rows - 1:rows, :]
        yield
        k_in = blocks(k_ref, F32)
        k = shift_lerp(k_in, prev_scr[1], mu_ref[:, 1:2, :])
        prev_scr[1] = k_in[:, rows - 1:rows, :]
        yield
        v_in = blocks(v_ref, F32)
        v = shift_lerp(v_in, prev_scr[2], mu_ref[:, 2:3, :])
        prev_scr[2] = v_in[:, rows - 1:rows, :]
        wa_in = wa_ref[0, rs, :]
        wa = shift_lerp(wa_in, prev_scr[3, 0, :, :LANES], mu_ref[0, 3:4, :LANES])
        prev_scr[3, 0, :, :LANES] = wa_in[rows - 1:rows, :]
        yield
        w_lo = jnp.tanh(wa[:, :LORA]).astype(BF16)
        a_lo = wa[:, LORA:].astype(BF16)
        w_raw = -jax.nn.softplus(-(w0 + per_block(_dot(w_lo, w2_ref[...])))) - 0.5
        logw = -jnp.exp(w_raw)
        yield
        a = jax.nn.sigmoid(a0 + per_block(_dot(a_lo, a2_ref[...])))
        kk = k * k_k
        kk = kk * lax.rsqrt(jnp.maximum(head_sum(kk * kk, 1), 1e-24))
        yield
        k = k * (1.0 + (a - 1.0) * k_a)
        bonus = head_sum(r * k * r_k, 1) * v
        yield
        cum = jnp.stack([_dot_exact_lhs(tri_blk, logw[b], 2) for b in range(nb)])
        yield
        cum3 = chunked(cum)
        cum_end = cum3[:, c - 1:c, :]
        e_neg = jnp.exp(-cum3)
        e_end = jnp.exp(cum_end - cum3)
        akk = chunked(a * kk)
        k3 = chunked(k)
        r_t = chunked(r) * jnp.exp(cum3)
        yield
        p_t = (-chunked(kk) * jnp.exp(cum3 - chunked(logw))).astype(BF16)
        p_s = _stack_heads(p_t)
        z_s = _stack_heads((akk * e_neg).astype(BF16))
        yield
        k_s = _stack_heads((k3 * e_neg).astype(BF16))
        zbar_s = _stack_heads((akk * e_end).astype(BF16))
        kbar_s = _stack_heads((k3 * e_end).astype(BF16))
        v_s = _stack_heads(chunked(v).astype(BF16))
        g_end = jnp.exp(cum_end)
        yield
        pr = jnp.concatenate([p_t, r_t.astype(BF16)], axis=1)
        zk = jnp.concatenate([z_s, k_s], axis=1)
        amat = _bdot_nt(pr, zk)
        l_pz = jnp.where(lower_strict, amat[:, :c, :wd], 0.0)
        l_pk = jnp.where(lower_strict, amat[:, :c, wd:], 0.0).astype(BF16)
        l_rz = jnp.where(lower_incl, amat[:, c:, :wd], 0.0).astype(BF16)
        l_rk = jnp.where(lower_incl, amat[:, c:, wd:], 0.0).astype(BF16)
        yield
        inv = (yield from _unit_lower_inverse(l_pz, c)).astype(BF16)
        lv = _bdot(l_pk, v_s).astype(BF16)
        yield
        tp_ui = _bdot(inv, jnp.concatenate([p_s, _stack_heads(lv)], axis=2)).astype(BF16)
        tp_ui_s = jnp.concatenate(
            [_stack_heads(tp_ui[:, :, :wd]), _stack_heads(tp_ui[:, :, wd:])], axis=2)
        yield
        rz = _bdot(l_rz, tp_ui_s)
        qeff = (r_t + rz[:, :, :wd]).astype(BF16)
        yi = rz[:, :, wd:] + _bdot(l_rk, v_s)
        yield
        mz = _bdot_tn(tp_ui_s, zbar_s)
        m_low = mz[:, :wd].astype(BF16)
        b_mat = mz[:, wd:] + _bdot_tn(v_s, kbar_s)
        yield
        s = state_scr[...]
        ys = []
        for ci in range(nc):
            s_b = s.astype(BF16)
            ys.append(_bdot_nt(of_chunk(qeff, ci), s_b) + of_chunk(yi, ci))
            s = s * of_chunk(g_end, ci) + _bdot(s_b, of_chunk(m_low, ci)) + of_chunk(b_mat, ci)
            yield
        state_scr[...] = s
        y = jnp.concatenate(ys, axis=1)
        mean = head_sum(y, 2) * (1.0 / HEAD_DIM)
        yc = y - mean
        yield
        var = head_sum(yc * yc, 1) * (1.0 / HEAD_DIM)
        y = yc * lax.rsqrt(var + RWKV_GN_EPS) * ln_g + ln_b + bonus
        yield
        y = y * _silu(blocks(g_ref, F32))
        for b in range(nb):
            o_ref[0, rs, b * wd:(b + 1) * wd] = y[b].astype(o_ref.dtype)

    _interleave([stream(si) for si in range(tc // rows)], RWKV_STREAM_SKEW)


def _rwkv(main, small, mu_rows, par_rows, w2, a2, batch, seq, *, tc=1024):
    nb = RWKV_BLOCKS
    wide = nb * RWKV_WIDTH
    ngrp = HEAD_W // wide
    return pl.pallas_call(
        functools.partial(_rwkv_kernel, tc=tc),
        grid=(batch, ngrp, seq // tc),
        in_specs=[
            pl.BlockSpec((1, tc, wide), lambda b, hg, ti: (b, ti, 4 * ngrp + hg)),
            pl.BlockSpec((1, tc, wide), lambda b, hg, ti: (b, ti, 5 * ngrp + hg)),
            pl.BlockSpec((1, tc, wide), lambda b, hg, ti: (b, ti, 6 * ngrp + hg)),
            pl.BlockSpec((1, tc, wide), lambda b, hg, ti: (b, ti, 7 * ngrp + hg)),
            pl.BlockSpec((1, tc, LANES), lambda b, hg, ti: (b, ti, 0)),
            pl.BlockSpec((nb, 8, RWKV_WIDTH), lambda b, hg, ti: (hg, 0, 0)),
            pl.BlockSpec((nb, 8, RWKV_WIDTH), lambda b, hg, ti: (hg, 0, 0)),
            pl.BlockSpec((LORA, wide), lambda b, hg, ti: (0, hg)),
            pl.BlockSpec((LORA, wide), lambda b, hg, ti: (0, hg)),
        ],
        out_specs=pl.BlockSpec((1, tc, wide), lambda b, hg, ti: (b, ti, hg)),
        out_shape=jax.ShapeDtypeStruct((batch, seq, HEAD_W), BF16),
        scratch_shapes=[
            pltpu.VMEM((nb, RWKV_WIDTH, RWKV_WIDTH), F32),
            pltpu.VMEM((4, nb, 1, RWKV_WIDTH), F32),
        ],
        compiler_params=pltpu.CompilerParams(
            dimension_semantics=("parallel", "parallel", "arbitrary"),
            vmem_limit_bytes=VMEM_LIMIT),
        name="rwkv7",
    )(main, main, main, main, small, mu_rows, par_rows, w2, a2)


def _sgu_kernel(u_ref, v_ref, g_ref, lng_ref, lnb_ref, ws_ref, bs_ref, o_ref, *, rows):
    c = GMLP_CHUNK
    causal = (lax.broadcasted_iota(jnp.int32, (c, c), 0)
              >= lax.broadcasted_iota(jnp.int32, (c, c), 1))
    v = _gelu(v_ref[...].astype(F32))
    mu = jnp.mean(v, axis=-1, keepdims=True)
    vc = v - mu
    var = jnp.mean(vc * vc, axis=-1, keepdims=True)
    vn = (vc * lax.rsqrt(var + LN_EPS) * lng_ref[...] + lnb_ref[...]).astype(BF16)
    for grp in range(GMLP_GROUPS):
        cs = slice(grp * c, (grp + 1) * c)
        w = jnp.where(causal, ws_ref[grp], 0.0).astype(BF16)
        bias = bs_ref[:, grp:grp + 1]
        for ch in range(rows // c):
            rs = slice(ch * c, (ch + 1) * c)
            mixed = _dot(w, vn[rs, cs]) + bias
            u = _gelu(u_ref[rs, cs].astype(F32))
            o_ref[rs, cs] = (u * mixed * _silu(g_ref[rs, cs].astype(F32))).astype(o_ref.dtype)


def _spatial_gating(proj, ln_g, ln_b, w_s, b_s_t, *, rows=512):
    m = proj.shape[0]
    d = D_MODEL
    return pl.pallas_call(
        functools.partial(_sgu_kernel, rows=rows),
        grid=(m // rows,),
        in_specs=[
            pl.BlockSpec((rows, d), lambda i: (i, 0)),
            pl.BlockSpec((rows, d), lambda i: (i, 1)),
            pl.BlockSpec((rows, d), lambda i: (i, 2)),
            pl.BlockSpec((1, d), lambda i: (0, 0)),
            pl.BlockSpec((1, d), lambda i: (0, 0)),
            pl.BlockSpec((GMLP_GROUPS, GMLP_CHUNK, GMLP_CHUNK), lambda i: (0, 0, 0)),
            pl.BlockSpec((GMLP_CHUNK, GMLP_GROUPS), lambda i: (0, 0)),
        ],
        out_specs=pl.BlockSpec((rows, d), lambda i: (i, 0)),
        out_shape=jax.ShapeDtypeStruct((m, d), BF16),
        compiler_params=pltpu.CompilerParams(
            dimension_semantics=("parallel",), vmem_limit_bytes=VMEM_LIMIT),
        name="spatial_gating",
    )(proj, proj, proj, ln_g, ln_b, w_s, b_s_t)


TAIL_SPLIT = 2


def _tail_kernel(a0_ref, a1_ref, h_ref, p_ref, w0_ref, w1_ref, gp_ref, wp_ref, wg_ref, o_ref):
    rows = h_ref.shape[0] // TAIL_SPLIT
    groups = [slice(i * rows, (i + 1) * rows) for i in range(TAIL_SPLIT)]
    ys = [_dot(a0_ref[rs, :], w0_ref[...]) + _dot(a1_ref[rs, :], w1_ref[...]) for rs in groups]
    for rs, y in zip(groups, ys):
        ms = jnp.mean(y * y, axis=-1, keepdims=True)
        h1 = h_ref[rs, :] + y * lax.rsqrt(ms + RMS_EPS) * gp_ref[...]
        gate = _dot(h1.astype(BF16), wg_ref[...])
        pp = _dot(p_ref[rs, :].astype(BF16), wp_ref[...])
        o_ref[rs, :] = h1 + pp * jax.nn.sigmoid(gate)


def _tail(act0, blk0, act1, blk1, h, p_all, layer, w_out, g_post, w_proj, w_gate, *, tm=512):
    m, d = h.shape
    p_row0 = layer * (m // tm)
    half = w_out.shape[0] // 2
    return pl.pallas_call(
        _tail_kernel,
        grid=(m // tm,),
        in_specs=[
            pl.BlockSpec((tm, half), lambda i: (i, blk0)),
            pl.BlockSpec((tm, half), lambda i: (i, blk1)),
            pl.BlockSpec((tm, d), lambda i: (i, 0)),
            pl.BlockSpec((tm, PLE_DIM), lambda i: (p_row0 + i, 0)),
            _resident((half, d), lambda i: (0, 0)),
            _resident((half, d), lambda i: (1, 0)),
            _resident((1, d), lambda i: (0, 0)),
            _resident((PLE_DIM, d), lambda i: (0, 0)),
            _resident((d, d), lambda i: (0, 0)),
        ],
        out_specs=pl.BlockSpec((tm, d), lambda i: (i, 0)),
        out_shape=jax.ShapeDtypeStruct((m, d), F32),
        compiler_params=pltpu.CompilerParams(
            dimension_semantics=("parallel",), vmem_limit_bytes=VMEM_LIMIT),
        name="tail",
    )(act0, act1, h, p_all, w_out, w_out, g_post, w_proj, w_gate)


def _per_block_rows(vecs):
    nblk = HEAD_W // RWKV_WIDTH
    rows = [v.reshape(nblk, 1, RWKV_WIDTH) for v in vecs]
    rows += [jnp.zeros((nblk, 1, RWKV_WIDTH), F32)] * (8 - len(rows))
    return jnp.concatenate(rows, axis=1)


def kernel(x, p, norm_pre, norm_post, ab_w_in, fox_f_bias, rwkv_mu, rwkv_w0, rwkv_w2, rwkv_a0, rwkv_a2, rwkv_k_k, rwkv_k_a, rwkv_r_k, rwkv_ln_g, rwkv_ln_b, ab_w_out, c_w_in, c_ln_g, c_ln_b, c_w_s, c_b_s, c_w_out, ple_w_proj, ple_w_gate):
    batch, seq, d = x.shape
    m = batch * seq
    w = HEAD_W
    h = x.reshape(m, d)

    w_in = ab_w_in[0]
    o_f = 3 * w
    o_ga = o_f + HEADS
    o_sh = o_ga + w
    o_lo = o_sh + 3 * w
    o_gb = o_lo + 2 * LORA
    w_bf = w_in.astype(BF16)
    w_main = jnp.concatenate([w_bf[:, :o_f], w_bf[:, o_ga:o_lo], w_bf[:, o_gb:]], axis=1)
    w_small = jnp.concatenate(
        [w_bf[:, o_lo:o_gb], w_bf[:, o_f:o_ga], jnp.zeros((d, LANES - HEADS), BF16)], axis=1)
    main, small = _norm_matmul(h, norm_pre[0:1], w_main, w_small)
    main3 = main.reshape(batch, seq, 8 * w)
    small3 = small.reshape(batch, seq, 2 * LANES)

    f_bias = jnp.concatenate([fox_f_bias[0], jnp.zeros((LANES - HEADS,), F32)]).reshape(1, LANES)
    fox_bias = _fox_bias(small3, f_bias, batch, seq)
    act_a = _fox_attention(main3, fox_bias, batch, seq)

    mu = rwkv_mu[0]
    mu_rows = _per_block_rows([mu[:w], mu[w:2 * w], mu[2 * w:3 * w]])
    mu_rows = mu_rows.at[:, 3, :2 * LORA].set(mu[3 * w:])
    par_rows = _per_block_rows([rwkv_w0[0], rwkv_a0[0], rwkv_k_k[0], rwkv_k_a[0],
                                rwkv_r_k[0].reshape(w), rwkv_ln_g[0], rwkv_ln_b[0]])
    act_b = _rwkv(main3, small3, mu_rows, par_rows, rwkv_w2[0].astype(BF16),
                  rwkv_a2[0].astype(BF16), batch, seq)

    p_all = p.reshape(p.shape[0] * m, PLE_DIM)
    h = _tail(act_a.reshape(m, w), 0, act_b.reshape(m, w), 0, h, p_all, 0,
              ab_w_out[0].astype(BF16), norm_post[0:1], ple_w_proj[0].astype(BF16),
              ple_w_gate[0].astype(BF16))

    proj = _norm_matmul(h, norm_pre[1:2], c_w_in[0].astype(BF16))[0]
    act_c = _spatial_gating(proj, c_ln_g[0:1], c_ln_b[0:1], c_w_s[0], c_b_s[0].T)
    h = _tail(act_c, 0, act_c, 1, h, p_all, 1,
              c_w_out[0].astype(BF16), norm_post[1:2], ple_w_proj[1].astype(BF16),
              ple_w_gate[1].astype(BF16))
    return h.reshape(batch, seq, d)
```
